```python
import jax, jax.numpy as jnp
from jax import lax
import numpy as np

D_MODEL = 2048
BATCH = 4
SEQ = 2048
DEPTH = 4
DEC_BATCH = 128
DEC_SEQ = 4
PAST_LEN = 16384
PAGE_SIZE = 128

D_A = D_MODEL // 2
CONV_A = 3
D_B = D_MODEL // 2
POOL_WINDOWS = (2, 4, 8, 16)
N_POOL = len(POOL_WINDOWS)
POOL_IN = D_B // N_POOL
POOL_OUT = D_MODEL // N_POOL
POOL_BUF = max(POOL_WINDOWS) - 1
D_C = D_MODEL // 2
CONV_C = 31
N_BRANCH = 3
D_FF = -(-(8 * D_MODEL) // (3 * 256)) * 256
D_IN = 3 * D_A + D_B + 2 * D_C + N_BRANCH * D_MODEL
SPLITS = (3 * D_A, 3 * D_A + D_B, 3 * D_A + D_B + 2 * D_C)
EPS = 1e-6

kernel_name = "hybrid_conv_pool_conformer_decoder_step"


def rmsnorm(x, g):
    xf = x.astype(jnp.float32)
    y = xf * lax.rsqrt(jnp.mean(xf * xf, axis=-1, keepdims=True) + EPS)
    return (y * g.astype(jnp.float32)).astype(x.dtype)


def layernorm(x, g, b):
    xf = x.astype(jnp.float32)
    mu = jnp.mean(xf, axis=-1, keepdims=True)
    var = jnp.mean(jnp.square(xf - mu), axis=-1, keepdims=True)
    y = (xf - mu) * lax.rsqrt(var + EPS)
    return (y * g.astype(jnp.float32) + b.astype(jnp.float32)).astype(x.dtype)


def causal_dwconv(buf, u, w):
    k_w, ch = w.shape
    xx = jnp.concatenate([buf, u], axis=1)
    y = lax.conv_general_dilated(
        xx, w[:, None, :].astype(xx.dtype), window_strides=(1,), padding="VALID",
        dimension_numbers=("NWC", "WIO", "NWC"), feature_group_count=ch)
    return y, xx[:, -(k_w - 1):]


def multiscale_pool(buf, u, p0, w_pool, scale):
    bsz, t_len, _ = u.shape
    xx = jnp.concatenate([buf, u], axis=1)
    xf = xx.astype(jnp.float32)
    cs = jnp.concatenate([jnp.zeros((bsz, 1, D_B), jnp.float32), jnp.cumsum(xf, axis=1)], axis=1)
    end = cs[:, POOL_BUF + 1:]
    pos = p0 + jnp.arange(t_len)
    outs = []
    for g, win in enumerate(POOL_WINDOWS):
        sl = slice(g * POOL_IN, (g + 1) * POOL_IN)
        start = cs[:, POOL_BUF + 1 - win:POOL_BUF + 1 - win + t_len, sl]
        cnt = jnp.minimum(pos + 1, win).astype(jnp.float32)[None, :, None]
        diff = (end[..., sl] - start) / cnt - xf[:, POOL_BUF:, sl]
        outs.append(diff.astype(u.dtype) @ w_pool[g])
    y = jnp.concatenate(outs, axis=-1) * scale
    return y, xx[:, -POOL_BUF:]


def setup_inputs(seed: int = 0) -> dict:
    key = jax.random.key(seed)
    ks = iter(jax.random.split(key, 40))

    def nrm(shape, s):
        return jax.random.normal(next(ks), shape, jnp.float32) * s

    return {
        "x_prompt": nrm((BATCH, SEQ, D_MODEL), 1.0),
        "x_sample": nrm((DEC_BATCH, DEC_SEQ, D_MODEL), 1.0),
        "c_prompt": nrm((BATCH, D_MODEL), 1.0),
        "c_sample": nrm((DEC_BATCH, D_MODEL), 1.0),
        "state_conv_a": nrm((DEPTH, DEC_BATCH, CONV_A - 1, D_A), 1.0),
        "state_pool": nrm((DEPTH, DEC_BATCH, POOL_BUF, D_B), 1.0),
        "state_conv_c": nrm((DEPTH, DEC_BATCH, CONV_C - 1, D_C), 1.0),
        "ada_w": nrm((DEPTH, D_MODEL, 6 * D_MODEL), 0.5 * D_MODEL ** -0.5),
        "ada_b": nrm((DEPTH, 6 * D_MODEL), 0.01),
        "norm_mix_g": 1.0 + nrm((DEPTH, D_MODEL), 0.01),
        "w_in": nrm((DEPTH, D_MODEL, D_IN), D_MODEL ** -0.5),
        "conv_a_w": nrm((DEPTH, CONV_A, D_A), CONV_A ** -0.5),
        "w_out_a": nrm((DEPTH, D_A, D_MODEL), D_A ** -0.5),
        "pool_w": nrm((DEPTH, N_POOL, POOL_IN, POOL_OUT), POOL_IN ** -0.5),
        "pool_scale": 1.0 + nrm((DEPTH, D_MODEL), 0.1),
        "conv_c_w": nrm((DEPTH, CONV_C, D_C), CONV_C ** -0.5),
        "conv_c_b": nrm((DEPTH, D_C), 0.01),
        "ln_c_g": 1.0 + nrm((DEPTH, D_C), 0.01),
        "ln_c_b": nrm((DEPTH, D_C), 0.01),
        "w_pw2": nrm((DEPTH, D_C, D_MODEL), D_C ** -0.5),
        "b_pw2": nrm((DEPTH, D_MODEL), 0.01),
        "w_o": nrm((DEPTH, D_MODEL, D_MODEL), D_MODEL ** -0.5),
        "norm_ffn_g": 1.0 + nrm((DEPTH, D_MODEL), 0.01),
        "w_gate_up": nrm((DEPTH, D_MODEL, 2 * D_FF), D_MODEL ** -0.5),
        "w_down": nrm((DEPTH, D_FF, D_MODEL), D_FF ** -0.5),
        "final_norm_g": 1.0 + nrm((D_MODEL,), 0.01),
    }


def reference(x_prompt, x_sample, c_prompt, c_sample, state_conv_a, state_pool, state_conv_c,
              ada_w, ada_b, norm_mix_g, w_in, conv_a_w, w_out_a, pool_w, pool_scale,
              conv_c_w, conv_c_b, ln_c_g, ln_c_b, w_pw2, b_pw2, w_o, norm_ffn_g,
              w_gate_up, w_down, final_norm_g):

    def mixer(h, st_a, st_p, st_c, p0, l):
        proj = h @ w_in[l]
        ua, ub, uc, ug = jnp.split(proj, SPLITS, axis=-1)
        b_gate, c_gate, v = jnp.split(ua, 3, axis=-1)
        conv_a, new_a = causal_dwconv(st_a, c_gate * v, conv_a_w[l])
        y_a = (b_gate * conv_a) @ w_out_a[l]
        y_b, new_p = multiscale_pool(st_p, ub, p0, pool_w[l], pool_scale[l])
        glu = uc[..., :D_C] * jax.nn.sigmoid(uc[..., D_C:])
        conv_c, new_c = causal_dwconv(st_c, glu, conv_c_w[l])
        z = jax.nn.silu(layernorm(conv_c + conv_c_b[l], ln_c_g[l], ln_c_b[l]))
        y_c = z @ w_pw2[l] + b_pw2[l]
        g = jax.nn.sigmoid(ug)
        merged = (g[..., :D_MODEL] * y_a + g[..., D_MODEL:2 * D_MODEL] * y_b
                  + g[..., 2 * D_MODEL:] * y_c)
        return merged @ w_o[l], new_a, new_p, new_c

    def swiglu(h, l):
        gu = h @ w_gate_up[l]
        return (jax.nn.silu(gu[..., :D_FF]) * gu[..., D_FF:]) @ w_down[l]

    def run_group(x, c, st_a, st_p, st_c, p0):
        new_a, new_p, new_c = [], [], []
        for l in range(DEPTH):
            mod = (jax.nn.silu(c) @ ada_w[l] + ada_b[l])[:, None, :]
            sh1, sc1, g1, sh2, sc2, g2 = jnp.split(mod, 6, axis=-1)
            h = rmsnorm(x, norm_mix_g[l]) * (1 + sc1) + sh1
            m, a_s, p_s, c_s = mixer(h, st_a[l], st_p[l], st_c[l], p0, l)
            x = x + g1 * m
            h = rmsnorm(x, norm_ffn_g[l]) * (1 + sc2) + sh2
            x = x + g2 * swiglu(h, l)
            new_a.append(a_s)
            new_p.append(p_s)
            new_c.append(c_s)
        return rmsnorm(x, final_norm_g), jnp.stack(new_a), jnp.stack(new_p), jnp.stack(new_c)

    dt = x_prompt.dtype
    zeros_a = jnp.zeros((DEPTH, BATCH, CONV_A - 1, D_A), dt)
    zeros_p = jnp.zeros((DEPTH, BATCH, POOL_BUF, D_B), dt)
    zeros_c = jnp.zeros((DEPTH, BATCH, CONV_C - 1, D_C), dt)
    y_prompt, na_p, np_p, nc_p = run_group(x_prompt, c_prompt, zeros_a, zeros_p, zeros_c, 0)
    y_sample, na_s, np_s, nc_s = run_group(x_sample, c_sample, state_conv_a, state_pool,
                                           state_conv_c, PAST_LEN)
    return (y_prompt, y_sample, na_p, np_p, nc_p, na_s, np_s, nc_s)
```

```python
import functools

import jax
import jax.numpy as jnp
from jax import lax
from jax.experimental import pallas as pl
from jax.experimental.pallas import tpu as pltpu

D_MODEL = 2048
BATCH = 4
SEQ = 2048
DEPTH = 4
DEC_BATCH = 128
DEC_SEQ = 4
PAST_LEN = 16384

D_A = D_MODEL // 2
CONV_A = 3
D_B = D_MODEL // 2
POOL_WINDOWS = (2, 4, 8, 16)
N_POOL = len(POOL_WINDOWS)
POOL_IN = D_B // N_POOL
POOL_OUT = D_MODEL // N_POOL
POOL_BUF = max(POOL_WINDOWS) - 1
D_C = D_MODEL // 2
CONV_C = 31
D_FF = 5632
EPS = 1e-6

F32 = jnp.float32
BF16 = jnp.bfloat16

V7X_SUBLANES = 8
V7X_LANES = 128
V7X_VREG_ELEMS = V7X_SUBLANES * V7X_LANES
V7X_VMEM_BYTES = 64 * 1024 * 1024
V7X_VMEM_USABLE_BYTES = V7X_VMEM_BYTES - 6 * 1024 * 1024

ROW_TILE = 512
COL_TILE = 512
MERGE_COL_TILE = 256
C_ALL_ROWS = 144
MOD_COL_TILE = 1024
ACC_VREGS = 32


def _round_up(n, m):
    return -(-n // m) * m


def _chunk_rows(width):
    return ACC_VREGS * V7X_VREG_ELEMS // width


def _params(vmem_bytes, n_axes):
    limit = min(V7X_VMEM_USABLE_BYTES, int(vmem_bytes))
    return pltpu.CompilerParams(dimension_semantics=("arbitrary",) * n_axes, vmem_limit_bytes=limit)


def _dot(a, b):
    return jnp.dot(a, b, preferred_element_type=F32)


def _rows(m, r0, rc):
    if m.shape[0] == 1:
        return m[...]
    n_seq = m.shape[0]
    if rc <= n_seq:
        return m[pl.ds(r0 % n_seq if isinstance(r0, int) else lax.rem(r0, n_seq), rc), :]
    raise NotImplementedError


def _adaln(x, gain, scale, shift):
    y = x * lax.rsqrt(jnp.mean(x * x, axis=-1, keepdims=True) + EPS)
    return (y * gain) * (1.0 + scale) + shift


def _for_chunks(tm, rc, body):
    def step(c, carry):
        body(pl.multiple_of(c * rc, rc))
        return carry
    lax.fori_loop(0, tm // rc, step, 0)


class Group:
    def __init__(self, name, n_seq, t_len, time_major, p0):
        self.name = name
        self.n_seq = n_seq
        self.t_len = t_len
        self.time_major = time_major
        self.p0 = p0
        self.rows = n_seq * t_len
        self.tm = ROW_TILE
        self.n_tiles = self.rows // self.tm
        self.stride = n_seq if time_major else 1
        self.tps = 1 if time_major else t_len // self.tm
        assert self.rows % self.tm == 0
        assert (time_major and self.n_tiles == 1) or (not time_major and t_len % self.tm == 0)

    def hist_rows(self, k_hist):
        return _round_up(k_hist * self.stride, V7X_SUBLANES)

    def tail_rows(self, k_hist):
        return self.tm if self.time_major else _round_up(k_hist, V7X_SUBLANES)

    def mod_spec(self, which, row_axis):
        if self.time_major:
            return pl.BlockSpec((None, self.n_seq, D_MODEL), lambda *g: (g[-1][0], 0, which))
        tps = self.tps
        return pl.BlockSpec((None, None, 1, D_MODEL),
                            lambda *g: (g[-1][0], g[row_axis] // tps, 0, which))

    def next_mod_spec(self, which, row_axis):
        nxt = lambda l_ref: jnp.minimum(l_ref[0] + 1, DEPTH - 1)
        if self.time_major:
            return pl.BlockSpec((None, self.n_seq, D_MODEL), lambda *g: (nxt(g[-1]), 0, which))
        tps = self.tps
        return pl.BlockSpec((None, None, 1, D_MODEL),
                            lambda *g: (nxt(g[-1]), g[row_axis] // tps, 0, which))


PROMPT = Group("prompt", BATCH, SEQ, False, 0)
SAMPLE = Group("sample", DEC_BATCH, DEC_SEQ, True, PAST_LEN)
SH1, SC1, G1, SH2, SC2, G2 = range(6)


def _call(kernel, grp_name, name, grid, in_specs, out_specs, out_shape, scratch, vmem, l_arr, args):
    grid_spec = pltpu.PrefetchScalarGridSpec(
        num_scalar_prefetch=1, grid=grid, in_specs=in_specs, out_specs=out_specs,
        scratch_shapes=scratch)
    return pl.pallas_call(kernel, grid_spec=grid_spec, out_shape=out_shape,
                          compiler_params=_params(vmem, len(grid)),
                          name=f"{name}_{grp_name}")(l_arr, *args)


def _mod_kernel(c_ref, w_ref, b_ref, o_ref):
    c = c_ref[...]
    a = (c * jax.nn.sigmoid(c)).astype(BF16)
    o_ref[...] = _dot(a, w_ref[...].astype(BF16)) + b_ref[...]


def _modulation(c_all, ada_w, ada_b):
    n_cols = 6 * D_MODEL
    tn = MOD_COL_TILE
    vmem = 2 * (D_MODEL * tn * 4) + D_MODEL * tn * 2 + 4 * C_ALL_ROWS * (D_MODEL + 2 * tn) * 4 + (4 << 20)
    return pl.pallas_call(
        _mod_kernel,
        grid=(DEPTH, n_cols // tn),
        in_specs=[
            pl.BlockSpec((C_ALL_ROWS, D_MODEL), lambda l, n: (0, 0)),
            pl.BlockSpec((None, D_MODEL, tn), lambda l, n: (l, 0, n)),
            pl.BlockSpec((None, 1, tn), lambda l, n: (l, 0, n)),
        ],
        out_specs=pl.BlockSpec((None, C_ALL_ROWS, tn), lambda l, n: (l, 0, n)),
        out_shape=jax.ShapeDtypeStruct((DEPTH, C_ALL_ROWS, n_cols), F32),
        compiler_params=_params(vmem, 2),
        name="modulation",
    )(c_all, ada_w, ada_b.reshape(DEPTH, 1, n_cols))


def _norm_kernel(l_ref, x_ref, gain_ref, *rest, tm, modulated):
    rc = _chunk_rows(D_MODEL)
    if modulated:
        sc_ref, sh_ref, o_ref = rest
    else:
        (o_ref,) = rest

    def body(r0):
        x = x_ref[pl.ds(r0, rc), :]
        if modulated:
            o_ref[pl.ds(r0, rc), :] = _adaln(x, gain_ref[...], _rows(sc_ref, r0, rc),
                                             _rows(sh_ref, r0, rc)).astype(BF16)
        else:
            o_ref[pl.ds(r0, rc), :] = (
                x * lax.rsqrt(jnp.mean(x * x, axis=-1, keepdims=True) + EPS) * gain_ref[...])
    _for_chunks(tm, rc, body)


def _norm(grp, l_arr, x, gain, mod):
    tm = grp.tm
    modulated = mod is not None
    row = pl.BlockSpec((tm, D_MODEL), lambda i, l: (i, 0))
    in_specs = [row, pl.BlockSpec((1, D_MODEL), lambda i, l: (0, 0))]
    args = [x, gain]
    if modulated:
        in_specs += [grp.mod_spec(SC1, 0), grp.mod_spec(SH1, 0)]
        args += [mod, mod]
    return _call(functools.partial(_norm_kernel, tm=tm, modulated=modulated), grp.name,
                 "norm_mod" if modulated else "norm_out", (grp.n_tiles,), in_specs, row,
                 jax.ShapeDtypeStruct((grp.rows, D_MODEL), BF16 if modulated else F32), [],
                 6 * tm * D_MODEL * 4 + (8 << 20), l_arr, args)


def _hist_begin(xx_ref, hp, tm, t_in_seq):
    @pl.when(t_in_seq == 0)
    def _():
        xx_ref[pl.ds(0, hp), :] = jnp.zeros((hp, xx_ref.shape[1]), F32)

    @pl.when(t_in_seq != 0)
    def _():
        xx_ref[pl.ds(0, hp), :] = xx_ref[pl.ds(tm, hp), :]


def _tap_rowmajor(xx_ref, hp, k_hist, r0, rc, cols=slice(None)):
    return lambda j: xx_ref[pl.ds(r0 + hp - (k_hist - j), rc), cols]


def _tap_timemajor(st_ref, new_ref, n_seq, k_hist, t, cols=slice(None)):
    def tap(j):
        step = t + j
        if step < k_hist:
            return st_ref[pl.ds(step * n_seq, n_seq), cols]
        return new_ref[pl.ds((step - k_hist) * n_seq, n_seq), cols]
    return tap


def _dwconv(tap, w_ref, k_w):
    acc = None
    for k in range(k_w):
        term = tap(k) * w_ref[k:k + 1, :]
        acc = term if acc is None else acc + term
    return acc


def _tail_store(grp, tail_ref, xx_ref, hp, k_hist, t_in_seq):
    r = grp.tail_rows(k_hist)

    @pl.when(t_in_seq == grp.tps - 1)
    def _():
        tail_ref[...] = xx_ref[pl.ds(hp + grp.tm - r, r), :]


def _tile_chunks(grp, width):
    if grp.time_major:
        assert grp.n_seq * width <= 4 * ACC_VREGS * V7X_VREG_ELEMS
        return [(t * grp.n_seq, grp.n_seq, t) for t in range(grp.t_len)]
    rc = _chunk_rows(width)
    return [(r0, rc, None) for r0 in range(0, grp.tm, rc)]


def _branch_a_kernel(l_ref, *refs, grp, tc):
    if grp.time_major:
        h_ref, wb_ref, wc_ref, wv_ref, cw_ref, st_ref, ya_ref, new_ref, bg_ref = refs
    else:
        h_ref, wb_ref, wc_ref, wv_ref, cw_ref, ya_ref, tail_ref, xx_ref, bg_ref = refs
    tm, k_hist = grp.tm, CONV_A - 1
    h = h_ref[...]
    bg_ref[...] = _dot(h, wb_ref[...])
    cv = _dot(h, wc_ref[...]) * _dot(h, wv_ref[...])
    if grp.time_major:
        new_ref[...] = cv
    else:
        t_in_seq = pl.program_id(1) % grp.tps
        hp = grp.hist_rows(k_hist)
        _hist_begin(xx_ref, hp, tm, t_in_seq)
        xx_ref[pl.ds(hp, tm), :] = cv
        _tail_store(grp, tail_ref, xx_ref, hp, k_hist, t_in_seq)
    for r0, rc, t in _tile_chunks(grp, tc):
        if grp.time_major:
            tap = _tap_timemajor(st_ref, new_ref, grp.n_seq, k_hist, t)
        else:
            tap = _tap_rowmajor(xx_ref, hp, k_hist, r0, rc)
        ya_ref[pl.ds(r0, rc), :] = (bg_ref[pl.ds(r0, rc), :] * _dwconv(tap, cw_ref, CONV_A)).astype(BF16)


def _branch_a(grp, l_arr, h, w_in, conv_w, state):
    tm, tc, k_hist = grp.tm, COL_TILE, CONV_A - 1
    nq = D_A // tc
    off_c, off_v = D_A // tc, 2 * D_A // tc
    in_specs = [
        pl.BlockSpec((tm, D_MODEL), lambda q, i, l: (i, 0)),
        pl.BlockSpec((None, D_MODEL, tc), lambda q, i, l: (l[0], 0, q)),
        pl.BlockSpec((None, D_MODEL, tc), lambda q, i, l: (l[0], 0, off_c + q)),
        pl.BlockSpec((None, D_MODEL, tc), lambda q, i, l: (l[0], 0, off_v + q)),
        pl.BlockSpec((None, CONV_A, tc), lambda q, i, l: (l[0], 0, q)),
    ]
    args = [h, w_in, w_in, w_in, conv_w]
    scratch = []
    if grp.time_major:
        in_specs.append(pl.BlockSpec((None, k_hist * grp.n_seq, tc), lambda q, i, l: (l[0], 0, q)))
        args.append(state)
        tail_spec = pl.BlockSpec((tm, tc), lambda q, i, l: (0, q))
        tail_shape = jax.ShapeDtypeStruct((tm, D_A), F32)
    else:
        tps = grp.tps
        r = grp.tail_rows(k_hist)
        scratch.append(pltpu.VMEM((grp.hist_rows(k_hist) + tm, tc), F32))
        tail_spec = pl.BlockSpec((None, r, tc), lambda q, i, l: (i // tps, 0, q))
        tail_shape = jax.ShapeDtypeStruct((grp.n_seq, r, D_A), F32)
    scratch.append(pltpu.VMEM((tm, tc), F32))
    vmem = 2 * (tm * D_MODEL * 2 + 3 * D_MODEL * tc * 2 + tm * tc * 2 + tm * tc * 4) + 10 * tm * tc * 4 + (6 << 20)
    return _call(functools.partial(_branch_a_kernel, grp=grp, tc=tc), grp.name, "branch_a",
                 (nq, grp.n_tiles), in_specs,
                 [pl.BlockSpec((tm, tc), lambda q, i, l: (i, q)), tail_spec],
                 [jax.ShapeDtypeStruct((grp.rows, D_A), BF16), tail_shape], scratch, vmem, l_arr, args)


def _branch_b_kernel(l_ref, *refs, grp):
    if grp.time_major:
        h_ref, w_ref, st_ref, diff_ref, new_ref = refs
    else:
        h_ref, w_ref, diff_ref, tail_ref, xx_ref = refs
    tm, k_hist = grp.tm, POOL_BUF
    u = _dot(h_ref[...], w_ref[...])
    if grp.time_major:
        new_ref[...] = u
    else:
        t_in_seq = pl.program_id(0) % grp.tps
        hp = grp.hist_rows(k_hist)
        _hist_begin(xx_ref, hp, tm, t_in_seq)
        xx_ref[pl.ds(hp, tm), :] = u
        _tail_store(grp, tail_ref, xx_ref, hp, k_hist, t_in_seq)
    for g, win in enumerate(POOL_WINDOWS):
        cols = slice(g * POOL_IN, (g + 1) * POOL_IN)
        for r0, rc, t in _tile_chunks(grp, POOL_IN):
            if grp.time_major:
                tap = _tap_timemajor(st_ref, new_ref, grp.n_seq, k_hist, t, cols)
                assert grp.p0 + 1 >= win
                cnt = float(win)
            else:
                tap = _tap_rowmajor(xx_ref, hp, k_hist, r0, rc, cols)
                pos = grp.p0 + t_in_seq * tm + r0 + lax.broadcasted_iota(jnp.int32, (rc, 1), 0)
                cnt = jnp.minimum(pos + 1, win).astype(F32)
            cur = tap(k_hist)
            total = cur
            for j in range(1, win):
                total = total + tap(k_hist - j)
            diff_ref[pl.ds(r0, rc), cols] = (total / cnt - cur).astype(BF16)


def _branch_b(grp, l_arr, h, w_in, state):
    tm, k_hist = grp.tm, POOL_BUF
    off = 3 * D_A // D_B
    in_specs = [
        pl.BlockSpec((tm, D_MODEL), lambda i, l: (i, 0)),
        pl.BlockSpec((None, D_MODEL, D_B), lambda i, l: (l[0], 0, off)),
    ]
    args = [h, w_in]
    scratch = []
    if grp.time_major:
        in_specs.append(pl.BlockSpec((None, k_hist * grp.n_seq, D_B), lambda i, l: (l[0], 0, 0)))
        args.append(state)
        tail_spec = pl.BlockSpec((tm, D_B), lambda i, l: (0, 0))
        tail_shape = jax.ShapeDtypeStruct((tm, D_B), F32)
        state_bytes = 2 * k_hist * grp.n_seq * D_B * 4
    else:
        tps = grp.tps
        r = grp.tail_rows(k_hist)
        scratch.append(pltpu.VMEM((grp.hist_rows(k_hist) + tm, D_B), F32))
        tail_spec = pl.BlockSpec((None, r, D_B), lambda i, l: (i // tps, 0, 0))
        tail_shape = jax.ShapeDtypeStruct((grp.n_seq, r, D_B), F32)
        state_bytes = 0
    vmem = (2 * (tm * D_MODEL * 2 + D_MODEL * D_B * 2 + tm * D_B * 2 + tm * D_B * 4) + 8 * tm * D_B * 4
            + state_bytes + (6 << 20))
    return _call(functools.partial(_branch_b_kernel, grp=grp), grp.name, "branch_b",
                 (grp.n_tiles,), in_specs,
                 [pl.BlockSpec((tm, D_B), lambda i, l: (i, 0)), tail_spec],
                 [jax.ShapeDtypeStruct((grp.rows, D_B), BF16), tail_shape], scratch, vmem, l_arr, args)


def _branch_c_kernel(l_ref, *refs, grp):
    if grp.time_major:
        h_ref, w1_ref, w2_ref, cw_ref, cb_ref, lg_ref, lb_ref, st_ref, z_ref, new_ref = refs
    else:
        h_ref, w1_ref, w2_ref, cw_ref, cb_ref, lg_ref, lb_ref, z_ref, tail_ref, xx_ref = refs
    tm, k_hist = grp.tm, CONV_C - 1
    h = h_ref[...]
    glu = _dot(h, w1_ref[...]) * jax.nn.sigmoid(_dot(h, w2_ref[...]))
    if grp.time_major:
        new_ref[...] = glu
    else:
        t_in_seq = pl.program_id(0) % grp.tps
        hp = grp.hist_rows(k_hist)
        _hist_begin(xx_ref, hp, tm, t_in_seq)
        xx_ref[pl.ds(hp, tm), :] = glu
        _tail_store(grp, tail_ref, xx_ref, hp, k_hist, t_in_seq)
    for r0, rc, t in _tile_chunks(grp, D_C):
        if grp.time_major:
            tap = _tap_timemajor(st_ref, new_ref, grp.n_seq, k_hist, t)
        else:
            tap = _tap_rowmajor(xx_ref, hp, k_hist, r0, rc)
        v = _dwconv(tap, cw_ref, CONV_C) + cb_ref[...]
        mu = jnp.mean(v, axis=-1, keepdims=True)
        vc = v - mu
        var = jnp.mean(vc * vc, axis=-1, keepdims=True)
        y = vc * lax.rsqrt(var + EPS) * lg_ref[...] + lb_ref[...]
        z_ref[pl.ds(r0, rc), :] = (y * jax.nn.sigmoid(y)).astype(BF16)


def _branch_c(grp, l_arr, h, w_in, conv_w, conv_b, ln_g, ln_b, state):
    tm, k_hist = grp.tm, CONV_C - 1
    off1 = (3 * D_A + D_B) // D_C
    vec = pl.BlockSpec((None, 1, D_C), lambda i, l: (l[0], 0, 0))
    in_specs = [
        pl.BlockSpec((tm, D_MODEL), lambda i, l: (i, 0)),
        pl.BlockSpec((None, D_MODEL, D_C), lambda i, l: (l[0], 0, off1)),
        pl.BlockSpec((None, D_MODEL, D_C), lambda i, l: (l[0], 0, off1 + 1)),
        pl.BlockSpec((None, CONV_C, D_C), lambda i, l: (l[0], 0, 0)),
        vec, vec, vec,
    ]
    args = [h, w_in, w_in, conv_w, conv_b, ln_g, ln_b]
    scratch = []
    if grp.time_major:
        in_specs.append(pl.BlockSpec((None, k_hist * grp.n_seq, D_C), lambda i, l: (l[0], 0, 0),
                                     pipeline_mode=pl.Buffered(1)))
        args.append(state)
        tail_spec = pl.BlockSpec((tm, D_C), lambda i, l: (0, 0))
        tail_shape = jax.ShapeDtypeStruct((tm, D_C), F32)
        state_bytes = k_hist * grp.n_seq * D_C * 4
    else:
        tps = grp.tps
        r = grp.tail_rows(k_hist)
        scratch.append(pltpu.VMEM((grp.hist_rows(k_hist) + tm, D_C), F32))
        tail_spec = pl.BlockSpec((None, r, D_C), lambda i, l: (i // tps, 0, 0))
        tail_shape = jax.ShapeDtypeStruct((grp.n_seq, r, D_C), F32)
        state_bytes = 0
    vmem = (2 * (tm * D_MODEL * 2 + 2 * D_MODEL * D_C * 2 + tm * D_C * 2 + tm * D_C * 4)
            + 10 * tm * D_C * 4 + state_bytes + (6 << 20))
    return _call(functools.partial(_branch_c_kernel, grp=grp), grp.name, "branch_c",
                 (grp.n_tiles,), in_specs,
                 [pl.BlockSpec((tm, D_C), lambda i, l: (i, 0)), tail_spec],
                 [jax.ShapeDtypeStruct((grp.rows, D_C), BF16), tail_shape], scratch, vmem, l_arr, args)


def _residual_epilogue(x_ref, gate_ref, acc_ref, ng_ref, sc_ref, sh_ref, xo_ref, ho_ref, tm):
    rc = _chunk_rows(D_MODEL)

    def body(r0):
        x = x_ref[pl.ds(r0, rc), :] + _rows(gate_ref, r0, rc) * acc_ref[pl.ds(r0, rc), :]
        xo_ref[pl.ds(r0, rc), :] = x
        ho_ref[pl.ds(r0, rc), :] = _adaln(x, ng_ref[...], _rows(sc_ref, r0, rc),
                                          _rows(sh_ref, r0, rc)).astype(BF16)
    _for_chunks(tm, rc, body)


def _merge_kernel(l_ref, h_ref, ya_ref, df_ref, z_ref, x_ref, g1_ref, sc2_ref, sh2_ref, ng_ref,
                  woa_ref, pw_ref, ps_ref, wp2_ref, bp2_ref, wg0_ref, wg1_ref, wg2_ref, wo_ref,
                  xo_ref, ho_ref, acc_ref, *, tm, nj):
    j = pl.program_id(1)
    h = h_ref[...]
    y_a = _dot(ya_ref[...], woa_ref[...])
    y_b = _dot(df_ref[...], pw_ref[...]) * ps_ref[...]
    y_c = _dot(z_ref[...], wp2_ref[...]) + bp2_ref[...]
    merged = (jax.nn.sigmoid(_dot(h, wg0_ref[...])) * y_a
              + jax.nn.sigmoid(_dot(h, wg1_ref[...])) * y_b
              + jax.nn.sigmoid(_dot(h, wg2_ref[...])) * y_c)
    part = _dot(merged.astype(BF16), wo_ref[...])

    @pl.when(j == 0)
    def _():
        acc_ref[...] = part

    @pl.when(j != 0)
    def _():
        acc_ref[...] += part

    @pl.when(j == nj - 1)
    def _():
        _residual_epilogue(x_ref, g1_ref, acc_ref, ng_ref, sc2_ref, sh2_ref, xo_ref, ho_ref, tm)


def _merge(grp, l_arr, h, ya, diffs, z, x, mod, norm_g, w_in, w_out_a, pool_w, pool_scale,
           w_pw2, b_pw2, w_o):
    tm, tn = grp.tm, MERGE_COL_TILE
    nj = D_MODEL // tn
    per_pool = POOL_OUT // tn
    off_g = (3 * D_A + D_B + 2 * D_C) // tn
    per_g = D_MODEL // tn
    row = lambda c: pl.BlockSpec((tm, c), lambda i, j, l: (i, 0))
    in_specs = [
        row(D_MODEL), row(D_A),
        pl.BlockSpec((tm, POOL_IN), lambda i, j, l: (i, j // per_pool)),
        row(D_C), row(D_MODEL),
        grp.mod_spec(G1, 0), grp.mod_spec(SC2, 0), grp.mod_spec(SH2, 0),
        pl.BlockSpec((None, 1, D_MODEL), lambda i, j, l: (l[0], 0, 0)),
        pl.BlockSpec((None, D_A, tn), lambda i, j, l: (l[0], 0, j)),
        pl.BlockSpec((None, None, POOL_IN, tn), lambda i, j, l: (l[0], j // per_pool, 0, j % per_pool)),
        pl.BlockSpec((None, 1, tn), lambda i, j, l: (l[0], 0, j)),
        pl.BlockSpec((None, D_C, tn), lambda i, j, l: (l[0], 0, j)),
        pl.BlockSpec((None, 1, tn), lambda i, j, l: (l[0], 0, j)),
        pl.BlockSpec((None, D_MODEL, tn), lambda i, j, l: (l[0], 0, off_g + j)),
        pl.BlockSpec((None, D_MODEL, tn), lambda i, j, l: (l[0], 0, off_g + per_g + j)),
        pl.BlockSpec((None, D_MODEL, tn), lambda i, j, l: (l[0], 0, off_g + 2 * per_g + j)),
        pl.BlockSpec((None, tn, D_MODEL), lambda i, j, l: (l[0], j, 0)),
    ]
    act_bytes = tm * (D_MODEL * 2 + D_A * 2 + POOL_IN * 2 + D_C * 2 + D_MODEL * 4)
    w_bytes = (D_A + POOL_IN + D_C + 3 * D_MODEL + D_MODEL) * tn * 2
    out_bytes = tm * D_MODEL * (4 + 2)
    vmem = 2 * (act_bytes + w_bytes + out_bytes) + tm * D_MODEL * 4 + 16 * tm * tn * 4 + (4 << 20)
    args = [h, ya, diffs, z, x, mod, mod, mod, norm_g, w_out_a, pool_w, pool_scale, w_pw2, b_pw2,
            w_in, w_in, w_in, w_o]
    return _call(functools.partial(_merge_kernel, tm=tm, nj=nj), grp.name, "merge",
                 (grp.n_tiles, nj), in_specs, [row(D_MODEL), row(D_MODEL)],
                 [jax.ShapeDtypeStruct((grp.rows, D_MODEL), F32),
                  jax.ShapeDtypeStruct((grp.rows, D_MODEL), BF16)],
                 [pltpu.VMEM((tm, D_MODEL), F32)], vmem, l_arr, args)


def _ffn_kernel(l_ref, h_ref, x_ref, g2_ref, wg_ref, wu_ref, wd_ref, ng_ref, sc_ref, sh_ref,
                xo_ref, ho_ref, acc_ref, *, tm, nk):
    k = pl.program_id(1)
    h = h_ref[...]
    gate = _dot(h, wg_ref[...])
    act = (gate * jax.nn.sigmoid(gate)) * _dot(h, wu_ref[...])
    part = _dot(act.astype(BF16), wd_ref[...])

    @pl.when(k == 0)
    def _():
        acc_ref[...] = part

    @pl.when(k != 0)
    def _():
        acc_ref[...] += part

    @pl.when(k == nk - 1)
    def _():
        _residual_epilogue(x_ref, g2_ref, acc_ref, ng_ref, sc_ref, sh_ref, xo_ref, ho_ref, tm)


def _ffn(grp, l_arr, h, x, mod, w_gate_up, w_down, norm_mix_g):
    tm, tn = grp.tm, COL_TILE
    nk = D_FF // tn
    row = pl.BlockSpec((tm, D_MODEL), lambda i, k, l: (i, 0))
    in_specs = [
        row, row, grp.mod_spec(G2, 0),
        pl.BlockSpec((None, D_MODEL, tn), lambda i, k, l: (l[0], 0, k)),
        pl.BlockSpec((None, D_MODEL, tn), lambda i, k, l: (l[0], 0, nk + k)),
        pl.BlockSpec((None, tn, D_MODEL), lambda i, k, l: (l[0], k, 0)),
        pl.BlockSpec((None, 1, D_MODEL), lambda i, k, l: (jnp.minimum(l[0] + 1, DEPTH - 1), 0, 0)),
        grp.next_mod_spec(SC1, 0), grp.next_mod_spec(SH1, 0),
    ]
    args = [h, x, mod, w_gate_up, w_gate_up, w_down, norm_mix_g, mod, mod]
    vmem = (2 * (tm * D_MODEL * (2 + 4) + 3 * D_MODEL * tn * 2 + tm * D_MODEL * (4 + 2))
            + tm * D_MODEL * 4 + 10 * tm * tn * 4 + (4 << 20))
    return _call(functools.partial(_ffn_kernel, tm=tm, nk=nk), grp.name, "ffn",
                 (grp.n_tiles, nk), in_specs, [row, row],
                 [jax.ShapeDtypeStruct((grp.rows, D_MODEL), F32),
                  jax.ShapeDtypeStruct((grp.rows, D_MODEL), BF16)],
                 [pltpu.VMEM((tm, D_MODEL), F32)], vmem, l_arr, args)


def _layer(grp, l_arr, x, h, mod, states, p):
    st_a, st_p, st_c = states if states is not None else (None, None, None)
    ya, tail_a = _branch_a(grp, l_arr, h, p["w_in"], p["conv_a_w"], st_a)
    diffs, tail_p = _branch_b(grp, l_arr, h, p["w_in"], st_p)
    z, tail_c = _branch_c(grp, l_arr, h, p["w_in"], p["conv_c_w"], p["conv_c_b"], p["ln_c_g"],
                          p["ln_c_b"], st_c)
    x, h = _merge(grp, l_arr, h, ya, diffs, z, x, mod, p["norm_ffn_g"], p["w_in"], p["w_out_a"],
                  p["pool_w"], p["pool_scale"], p["w_pw2"], p["b_pw2"], p["w_o"])
    x, h = _ffn(grp, l_arr, h, x, mod, p["w_gate_up"], p["w_down"], p["norm_mix_g"])
    return x, h, (tail_a, tail_p, tail_c)


def kernel(x_prompt, x_sample, c_prompt, c_sample, state_conv_a, state_pool, state_conv_c,
           ada_w, ada_b, norm_mix_g, w_in, conv_a_w, w_out_a, pool_w, pool_scale,
           conv_c_w, conv_c_b, ln_c_g, ln_c_b, w_pw2, b_pw2, w_o, norm_ffn_g,
           w_gate_up, w_down, final_norm_g):
    vec3 = lambda a: a.reshape(DEPTH, 1, a.shape[-1])
    p = {
        "w_in": w_in.astype(BF16), "w_out_a": w_out_a.astype(BF16), "pool_w": pool_w.astype(BF16),
        "w_pw2": w_pw2.astype(BF16), "w_o": w_o.astype(BF16), "w_gate_up": w_gate_up.astype(BF16),
        "w_down": w_down.astype(BF16),
        "conv_a_w": conv_a_w, "conv_c_w": conv_c_w,
        "conv_c_b": vec3(conv_c_b), "ln_c_g": vec3(ln_c_g), "ln_c_b": vec3(ln_c_b),
        "pool_scale": vec3(pool_scale), "b_pw2": vec3(b_pw2), "norm_ffn_g": vec3(norm_ffn_g),
        "norm_mix_g": vec3(norm_mix_g),
    }

    c_all = jnp.concatenate(
        [c_sample, c_prompt, jnp.zeros((C_ALL_ROWS - DEC_BATCH - BATCH, D_MODEL), F32)], axis=0)
    mod_s = _modulation(c_all, ada_w, ada_b)
    mod_p = mod_s[:, DEC_BATCH:DEC_BATCH + BATCH].reshape(DEPTH, BATCH, 1, 6 * D_MODEL)

    def time_major(s):
        s = jnp.swapaxes(s, -3, -2)
        return s.reshape(s.shape[:-3] + (-1, s.shape[-1]))

    states_s = (time_major(state_conv_a), time_major(state_pool), time_major(state_conv_c))
    x_p = x_prompt.reshape(BATCH * SEQ, D_MODEL)
    x_s = time_major(x_sample)

    l0 = jnp.zeros((1,), jnp.int32)
    h_p = _norm(PROMPT, l0, x_p, norm_mix_g[0:1], mod_p)
    h_s = _norm(SAMPLE, l0, x_s, norm_mix_g[0:1], mod_s)

    def body(carry, l):
        x_p, h_p, x_s, h_s = carry
        l_arr = jnp.reshape(l, (1,)).astype(jnp.int32)
        x_p, h_p, tails_p = _layer(PROMPT, l_arr, x_p, h_p, mod_p, None, p)
        x_s, h_s, tails_s = _layer(SAMPLE, l_arr, x_s, h_s, mod_s, states_s, p)
        return (x_p, h_p, x_s, h_s), (tails_p, tails_s)

    (x_p, _, x_s, _), (tails_p, tails_s) = lax.scan(body, (x_p, h_p, x_s, h_s), jnp.arange(DEPTH))
    y_p = _norm(PROMPT, l0, x_p, final_norm_g.reshape(1, D_MODEL), None)
    y_s = _norm(SAMPLE, l0, x_s, final_norm_g.reshape(1, D_MODEL), None)

    def from_time_major(a):
        a = a.reshape(a.shape[:-2] + (DEC_SEQ, DEC_BATCH, a.shape[-1]))
        return jnp.swapaxes(a, -3, -2)

    hist = (CONV_A - 1, POOL_BUF, CONV_C - 1)
    old_s = (state_conv_a, state_pool, state_conv_c)
    new_p = [tails_p[b][:, :, -hist[b]:, :] for b in range(3)]
    new_s = [jnp.concatenate([old_s[b], from_time_major(tails_s[b])], axis=2)[:, :, -hist[b]:, :]
             for b in range(3)]
    return (y_p.reshape(BATCH, SEQ, D_MODEL), from_time_major(y_s),
            new_p[0], new_p[1], new_p[2], new_s[0], new_s[1], new_s[2])
```

```python
import functools

import jax
import jax.numpy as jnp
from jax import lax
from jax.experimental import pallas as pl
from jax.experimental.pallas import tpu as pltpu

D_MODEL = 2048
BATCH = 4
SEQ = 2048
DEPTH = 4
DEC_BATCH = 128
DEC_SEQ = 4
PAST_LEN = 16384

D_A = D_MODEL // 2
CONV_A = 3
D_B = D_MODEL // 2
POOL_WINDOWS = (2, 4, 8, 16)
N_POOL = len(POOL_WINDOWS)
POOL_IN = D_B // N_POOL
POOL_OUT = D_MODEL // N_POOL
POOL_BUF = max(POOL_WINDOWS) - 1
D_C = D_MODEL // 2
CONV_C = 31
D_FF = 5632
EPS = 1e-6

F32 = jnp.float32
BF16 = jnp.bfloat16

V7X_SUBLANES = 8
V7X_LANES = 128
V7X_VREG_ELEMS = V7X_SUBLANES * V7X_LANES
V7X_VMEM_BYTES = 64 * 1024 * 1024
V7X_VMEM_USABLE_BYTES = V7X_VMEM_BYTES - 6 * 1024 * 1024

ROW_TILE = 512
COL_TILE = 512
MERGE_COL_TILE = 256
C_ALL_ROWS = 144
MOD_COL_TILE = 1024
ACC_VREGS = 32
EPILOGUE_UNROLL = 4
FFN_SUB_TILE = 256


def _round_up(n, m):
    return -(-n // m) * m


def _chunk_rows(width):
    return ACC_VREGS * V7X_VREG_ELEMS // width


def _params(vmem_bytes, n_axes):
    limit = min(V7X_VMEM_USABLE_BYTES, int(vmem_bytes))
    return pltpu.CompilerParams(dimension_semantics=("arbitrary",) * n_axes, vmem_limit_bytes=limit)


def _dot(a, b):
    return jnp.dot(a, b, preferred_element_type=F32)


def _rows(m, r0, rc):
    if m.shape[0] == 1:
        return m[...]
    n_seq = m.shape[0]
    if rc <= n_seq:
        return m[pl.ds(r0 % n_seq if isinstance(r0, int) else lax.rem(r0, n_seq), rc), :]
    raise NotImplementedError


def _adaln(x, gain, scale, shift):
    y = x * lax.rsqrt(jnp.mean(x * x, axis=-1, keepdims=True) + EPS)
    return (y * gain) * (1.0 + scale) + shift


def _for_chunks(tm, rc, body):
    def step(c, carry):
        body(pl.multiple_of(c * rc, rc))
        return carry
    lax.fori_loop(0, tm // rc, step, 0)


class Group:
    def __init__(self, name, n_seq, t_len, time_major, p0):
        self.name = name
        self.n_seq = n_seq
        self.t_len = t_len
        self.time_major = time_major
        self.p0 = p0
        self.rows = n_seq * t_len
        self.tm = ROW_TILE
        self.n_tiles = self.rows // self.tm
        self.stride = n_seq if time_major else 1
        self.tps = 1 if time_major else t_len // self.tm
        assert self.rows % self.tm == 0
        assert (time_major and self.n_tiles == 1) or (not time_major and t_len % self.tm == 0)

    def hist_rows(self, k_hist):
        return _round_up(k_hist * self.stride, V7X_SUBLANES)

    def tail_rows(self, k_hist):
        return self.tm if self.time_major else _round_up(k_hist, V7X_SUBLANES)

    def mod_spec(self, which, row_axis):
        if self.time_major:
            return pl.BlockSpec((None, self.n_seq, D_MODEL), lambda *g: (g[-1][0], 0, which))
        tps = self.tps
        return pl.BlockSpec((None, None, 1, D_MODEL),
                            lambda *g: (g[-1][0], g[row_axis] // tps, 0, which))

    def next_mod_spec(self, which, row_axis):
        nxt = lambda l_ref: jnp.minimum(l_ref[0] + 1, DEPTH - 1)
        if self.time_major:
            return pl.BlockSpec((None, self.n_seq, D_MODEL), lambda *g: (nxt(g[-1]), 0, which))
        tps = self.tps
        return pl.BlockSpec((None, None, 1, D_MODEL),
                            lambda *g: (nxt(g[-1]), g[row_axis] // tps, 0, which))


PROMPT = Group("prompt", BATCH, SEQ, False, 0)
SAMPLE = Group("sample", DEC_BATCH, DEC_SEQ, True, PAST_LEN)
SH1, SC1, G1, SH2, SC2, G2 = range(6)


def _call(kernel, grp_name, name, grid, in_specs, out_specs, out_shape, scratch, vmem, l_arr, args):
    grid_spec = pltpu.PrefetchScalarGridSpec(
        num_scalar_prefetch=1, grid=grid, in_specs=in_specs, out_specs=out_specs,
        scratch_shapes=scratch)
    return pl.pallas_call(kernel, grid_spec=grid_spec, out_shape=out_shape,
                          compiler_params=_params(vmem, len(grid)),
                          name=f"{name}_{grp_name}")(l_arr, *args)


def _mod_kernel(c_ref, w_ref, b_ref, o_ref):
    c = c_ref[...]
    a = (c * jax.nn.sigmoid(c)).astype(BF16)
    o_ref[...] = _dot(a, w_ref[...].astype(BF16)) + b_ref[...]


def _modulation(c_all, ada_w, ada_b):
    n_cols = 6 * D_MODEL
    tn = MOD_COL_TILE
    vmem = 2 * (D_MODEL * tn * 4) + D_MODEL * tn * 2 + 4 * C_ALL_ROWS * (D_MODEL + 2 * tn) * 4 + (4 << 20)
    return pl.pallas_call(
        _mod_kernel,
        grid=(DEPTH, n_cols // tn),
        in_specs=[
            pl.BlockSpec((C_ALL_ROWS, D_MODEL), lambda l, n: (0, 0)),
            pl.BlockSpec((None, D_MODEL, tn), lambda l, n: (l, 0, n)),
            pl.BlockSpec((None, 1, tn), lambda l, n: (l, 0, n)),
        ],
        out_specs=pl.BlockSpec((None, C_ALL_ROWS, tn), lambda l, n: (l, 0, n)),
        out_shape=jax.ShapeDtypeStruct((DEPTH, C_ALL_ROWS, n_cols), F32),
        compiler_params=_params(vmem, 2),
        name="modulation",
    )(c_all, ada_w, ada_b.reshape(DEPTH, 1, n_cols))


def _norm_kernel(l_ref, x_ref, gain_ref, *rest, tm, modulated):
    rc = _chunk_rows(D_MODEL)
    if modulated:
        sc_ref, sh_ref, o_ref = rest
    else:
        (o_ref,) = rest

    def body(r0):
        x = x_ref[pl.ds(r0, rc), :]
        if modulated:
            o_ref[pl.ds(r0, rc), :] = _adaln(x, gain_ref[...], _rows(sc_ref, r0, rc),
                                             _rows(sh_ref, r0, rc)).astype(BF16)
        else:
            o_ref[pl.ds(r0, rc), :] = (
                x * lax.rsqrt(jnp.mean(x * x, axis=-1, keepdims=True) + EPS) * gain_ref[...])
    _for_chunks(tm, rc, body)


def _norm(grp, l_arr, x, gain, mod):
    tm = grp.tm
    modulated = mod is not None
    row = pl.BlockSpec((tm, D_MODEL), lambda i, l: (i, 0))
    in_specs = [row, pl.BlockSpec((1, D_MODEL), lambda i, l: (0, 0))]
    args = [x, gain]
    if modulated:
        in_specs += [grp.mod_spec(SC1, 0), grp.mod_spec(SH1, 0)]
        args += [mod, mod]
    return _call(functools.partial(_norm_kernel, tm=tm, modulated=modulated), grp.name,
                 "norm_mod" if modulated else "norm_out", (grp.n_tiles,), in_specs, row,
                 jax.ShapeDtypeStruct((grp.rows, D_MODEL), BF16 if modulated else F32), [],
                 6 * tm * D_MODEL * 4 + (8 << 20), l_arr, args)


def _hist_begin(xx_ref, hp, tm, t_in_seq):
    @pl.when(t_in_seq == 0)
    def _():
        xx_ref[pl.ds(0, hp), :] = jnp.zeros((hp, xx_ref.shape[1]), F32)

    @pl.when(t_in_seq != 0)
    def _():
        xx_ref[pl.ds(0, hp), :] = xx_ref[pl.ds(tm, hp), :]


def _tap_rowmajor(xx_ref, hp, k_hist, r0, rc, cols=slice(None)):
    return lambda j: xx_ref[pl.ds(r0 + hp - (k_hist - j), rc), cols]


def _tap_timemajor(st_ref, new_ref, n_seq, k_hist, t, cols=slice(None)):
    def tap(j):
        step = t + j
        if step < k_hist:
            return st_ref[pl.ds(step * n_seq, n_seq), cols]
        return new_ref[pl.ds((step - k_hist) * n_seq, n_seq), cols]
    return tap


def _dwconv(tap, w_ref, k_w):
    acc = None
    for k in range(k_w):
        term = tap(k) * w_ref[k:k + 1, :]
        acc = term if acc is None else acc + term
    return acc


def _tail_store(grp, tail_ref, xx_ref, hp, k_hist, t_in_seq):
    r = grp.tail_rows(k_hist)

    @pl.when(t_in_seq == grp.tps - 1)
    def _():
        tail_ref[...] = xx_ref[pl.ds(hp + grp.tm - r, r), :]


def _tile_chunks(grp, width):
    if grp.time_major:
        assert grp.n_seq * width <= 4 * ACC_VREGS * V7X_VREG_ELEMS
        return [(t * grp.n_seq, grp.n_seq, t) for t in range(grp.t_len)]
    rc = _chunk_rows(width)
    return [(r0, rc, None) for r0 in range(0, grp.tm, rc)]


def _branch_a_kernel(l_ref, *refs, grp, tc):
    if grp.time_major:
        h_ref, wb_ref, wc_ref, wv_ref, cw_ref, st_ref, ya_ref, new_ref, bg_ref = refs
    else:
        h_ref, wb_ref, wc_ref, wv_ref, cw_ref, ya_ref, tail_ref, xx_ref, bg_ref = refs
    tm, k_hist = grp.tm, CONV_A - 1
    h = h_ref[...]
    bg_ref[...] = _dot(h, wb_ref[...])
    cv = _dot(h, wc_ref[...]) * _dot(h, wv_ref[...])
    if grp.time_major:
        new_ref[...] = cv
    else:
        t_in_seq = pl.program_id(1) % grp.tps
        hp = grp.hist_rows(k_hist)
        _hist_begin(xx_ref, hp, tm, t_in_seq)
        xx_ref[pl.ds(hp, tm), :] = cv
        _tail_store(grp, tail_ref, xx_ref, hp, k_hist, t_in_seq)
    for r0, rc, t in _tile_chunks(grp, tc):
        if grp.time_major:
            tap = _tap_timemajor(st_ref, new_ref, grp.n_seq, k_hist, t)
        else:
            tap = _tap_rowmajor(xx_ref, hp, k_hist, r0, rc)
        ya_ref[pl.ds(r0, rc), :] = (bg_ref[pl.ds(r0, rc), :] * _dwconv(tap, cw_ref, CONV_A)).astype(BF16)


def _branch_a(grp, l_arr, h, w_in, conv_w, state):
    tm, tc, k_hist = grp.tm, COL_TILE, CONV_A - 1
    nq = D_A // tc
    off_c, off_v = D_A // tc, 2 * D_A // tc
    in_specs = [
        pl.BlockSpec((tm, D_MODEL), lambda q, i, l: (i, 0)),
        pl.BlockSpec((None, D_MODEL, tc), lambda q, i, l: (l[0], 0, q)),
        pl.BlockSpec((None, D_MODEL, tc), lambda q, i, l: (l[0], 0, off_c + q)),
        pl.BlockSpec((None, D_MODEL, tc), lambda q, i, l: (l[0], 0, off_v + q)),
        pl.BlockSpec((None, CONV_A, tc), lambda q, i, l: (l[0], 0, q)),
    ]
    args = [h, w_in, w_in, w_in, conv_w]
    scratch = []
    if grp.time_major:
        in_specs.append(pl.BlockSpec((None, k_hist * grp.n_seq, tc), lambda q, i, l: (l[0], 0, q)))
        args.append(state)
        tail_spec = pl.BlockSpec((tm, tc), lambda q, i, l: (0, q))
        tail_shape = jax.ShapeDtypeStruct((tm, D_A), F32)
    else:
        tps = grp.tps
        r = grp.tail_rows(k_hist)
        scratch.append(pltpu.VMEM((grp.hist_rows(k_hist) + tm, tc), F32))
        tail_spec = pl.BlockSpec((None, r, tc), lambda q, i, l: (i // tps, 0, q))
        tail_shape = jax.ShapeDtypeStruct((grp.n_seq, r, D_A), F32)
    scratch.append(pltpu.VMEM((tm, tc), F32))
    vmem = 2 * (tm * D_MODEL * 2 + 3 * D_MODEL * tc * 2 + tm * tc * 2 + tm * tc * 4) + 10 * tm * tc * 4 + (6 << 20)
    return _call(functools.partial(_branch_a_kernel, grp=grp, tc=tc), grp.name, "branch_a",
                 (nq, grp.n_tiles), in_specs,
                 [pl.BlockSpec((tm, tc), lambda q, i, l: (i, q)), tail_spec],
                 [jax.ShapeDtypeStruct((grp.rows, D_A), BF16), tail_shape], scratch, vmem, l_arr, args)


def _branch_b_kernel(l_ref, *refs, grp):
    if grp.time_major:
        h_ref, w_ref, st_ref, diff_ref, new_ref = refs
    else:
        h_ref, w_ref, diff_ref, tail_ref, xx_ref = refs
    tm, k_hist = grp.tm, POOL_BUF
    u = _dot(h_ref[...], w_ref[...])
    if grp.time_major:
        new_ref[...] = u
    else:
        t_in_seq = pl.program_id(0) % grp.tps
        hp = grp.hist_rows(k_hist)
        _hist_begin(xx_ref, hp, tm, t_in_seq)
        xx_ref[pl.ds(hp, tm), :] = u
        _tail_store(grp, tail_ref, xx_ref, hp, k_hist, t_in_seq)
    for g, win in enumerate(POOL_WINDOWS):
        cols = slice(g * POOL_IN, (g + 1) * POOL_IN)
        for r0, rc, t in _tile_chunks(grp, POOL_IN):
            if grp.time_major:
                tap = _tap_timemajor(st_ref, new_ref, grp.n_seq, k_hist, t, cols)
                assert grp.p0 + 1 >= win
                cnt = float(win)
            else:
                tap = _tap_rowmajor(xx_ref, hp, k_hist, r0, rc, cols)
                pos = grp.p0 + t_in_seq * tm + r0 + lax.broadcasted_iota(jnp.int32, (rc, 1), 0)
                cnt = jnp.minimum(pos + 1, win).astype(F32)
            cur = tap(k_hist)
            total = cur
            for j in range(1, win):
                total = total + tap(k_hist - j)
            diff_ref[pl.ds(r0, rc), cols] = (total / cnt - cur).astype(BF16)


def _branch_b(grp, l_arr, h, w_in, state):
    tm, k_hist = grp.tm, POOL_BUF
    off = 3 * D_A // D_B
    in_specs = [
        pl.BlockSpec((tm, D_MODEL), lambda i, l: (i, 0)),
        pl.BlockSpec((None, D_MODEL, D_B), lambda i, l: (l[0], 0, off)),
    ]
    args = [h, w_in]
    scratch = []
    if grp.time_major:
        in_specs.append(pl.BlockSpec((None, k_hist * grp.n_seq, D_B), lambda i, l: (l[0], 0, 0)))
        args.append(state)
        tail_spec = pl.BlockSpec((tm, D_B), lambda i, l: (0, 0))
        tail_shape = jax.ShapeDtypeStruct((tm, D_B), F32)
        state_bytes = 2 * k_hist * grp.n_seq * D_B * 4
    else:
        tps = grp.tps
        r = grp.tail_rows(k_hist)
        scratch.append(pltpu.VMEM((grp.hist_rows(k_hist) + tm, D_B), F32))
        tail_spec = pl.BlockSpec((None, r, D_B), lambda i, l: (i // tps, 0, 0))
        tail_shape = jax.ShapeDtypeStruct((grp.n_seq, r, D_B), F32)
        state_bytes = 0
    vmem = (2 * (tm * D_MODEL * 2 + D_MODEL * D_B * 2 + tm * D_B * 2 + tm * D_B * 4) + 8 * tm * D_B * 4
            + state_bytes + (6 << 20))
    return _call(functools.partial(_branch_b_kernel, grp=grp), grp.name, "branch_b",
                 (grp.n_tiles,), in_specs,
                 [pl.BlockSpec((tm, D_B), lambda i, l: (i, 0)), tail_spec],
                 [jax.ShapeDtypeStruct((grp.rows, D_B), BF16), tail_shape], scratch, vmem, l_arr, args)


def _branch_c_kernel(l_ref, *refs, grp):
    if grp.time_major:
        h_ref, w1_ref, w2_ref, cw_ref, cb_ref, lg_ref, lb_ref, st_ref, z_ref, new_ref = refs
    else:
        h_ref, w1_ref, w2_ref, cw_ref, cb_ref, lg_ref, lb_ref, z_ref, tail_ref, xs_ref = refs
    tm, k_hist = grp.tm, CONV_C - 1
    h = h_ref[...]
    glu = _dot(h, w1_ref[...]) * jax.nn.sigmoid(_dot(h, w2_ref[...]))
    if grp.time_major:
        new_ref[...] = glu
    else:
        t_in_seq = pl.program_id(0) % grp.tps
        hp = grp.hist_rows(k_hist)
        x0 = xs_ref.at[0]
        _hist_begin(x0, hp, tm, t_in_seq)
        x0[pl.ds(hp, tm), :] = glu
        x0[pl.ds(hp + tm, V7X_SUBLANES), :] = jnp.zeros((V7X_SUBLANES, D_C), F32)
        _tail_store(grp, tail_ref, x0, hp, k_hist, t_in_seq)
        rc_copy = _chunk_rows(D_C)
        for b in range(1, V7X_SUBLANES):
            for r0 in range(0, hp + tm, rc_copy):
                xs_ref[b, pl.ds(r0, rc_copy), :] = x0[pl.ds(r0 + b, rc_copy), :]
    for r0, rc, t in _tile_chunks(grp, D_C):
        if grp.time_major:
            tap = _tap_timemajor(st_ref, new_ref, grp.n_seq, k_hist, t)
        else:
            def tap(k, r0=r0, rc=rc):
                a, b = divmod(hp - k_hist + k, V7X_SUBLANES)
                return xs_ref[b, pl.ds(r0 + V7X_SUBLANES * a, rc), :]
        v = _dwconv(tap, cw_ref, CONV_C) + cb_ref[...]
        mu = jnp.mean(v, axis=-1, keepdims=True)
        vc = v - mu
        var = jnp.mean(vc * vc, axis=-1, keepdims=True)
        y = vc * lax.rsqrt(var + EPS) * lg_ref[...] + lb_ref[...]
        z_ref[pl.ds(r0, rc), :] = (y * jax.nn.sigmoid(y)).astype(BF16)


def _branch_c(grp, l_arr, h, w_in, conv_w, conv_b, ln_g, ln_b, state):
    tm, k_hist = grp.tm, CONV_C - 1
    off1 = (3 * D_A + D_B) // D_C
    vec = pl.BlockSpec((None, 1, D_C), lambda i, l: (l[0], 0, 0))
    in_specs = [
        pl.BlockSpec((tm, D_MODEL), lambda i, l: (i, 0)),
        pl.BlockSpec((None, D_MODEL, D_C), lambda i, l: (l[0], 0, off1)),
        pl.BlockSpec((None, D_MODEL, D_C), lambda i, l: (l[0], 0, off1 + 1)),
        pl.BlockSpec((None, CONV_C, D_C), lambda i, l: (l[0], 0, 0)),
        vec, vec, vec,
    ]
    args = [h, w_in, w_in, conv_w, conv_b, ln_g, ln_b]
    scratch = []
    if grp.time_major:
        in_specs.append(pl.BlockSpec((None, k_hist * grp.n_seq, D_C), lambda i, l: (l[0], 0, 0),
                                     pipeline_mode=pl.Buffered(1)))
        args.append(state)
        tail_spec = pl.BlockSpec((tm, D_C), lambda i, l: (0, 0))
        tail_shape = jax.ShapeDtypeStruct((tm, D_C), F32)
        state_bytes = k_hist * grp.n_seq * D_C * 4
    else:
        tps = grp.tps
        r = grp.tail_rows(k_hist)
        shifted_rows = grp.hist_rows(k_hist) + tm + V7X_SUBLANES
        scratch.append(pltpu.VMEM((V7X_SUBLANES, shifted_rows, D_C), F32))
        tail_spec = pl.BlockSpec((None, r, D_C), lambda i, l: (i // tps, 0, 0))
        tail_shape = jax.ShapeDtypeStruct((grp.n_seq, r, D_C), F32)
        state_bytes = V7X_SUBLANES * shifted_rows * D_C * 4
    vmem = (2 * (tm * D_MODEL * 2 + 2 * D_MODEL * D_C * 2 + tm * D_C * 2 + tm * D_C * 4)
            + 10 * tm * D_C * 4 + state_bytes + (6 << 20))
    return _call(functools.partial(_branch_c_kernel, grp=grp), grp.name, "branch_c",
                 (grp.n_tiles,), in_specs,
                 [pl.BlockSpec((tm, D_C), lambda i, l: (i, 0)), tail_spec],
                 [jax.ShapeDtypeStruct((grp.rows, D_C), BF16), tail_shape], scratch, vmem, l_arr, args)


def _residual_epilogue(x_ref, gate_ref, acc_ref, ng_ref, sc_ref, sh_ref, xo_ref, ho_ref, tm):
    rc = _chunk_rows(D_MODEL)

    def body(r_base):
        for u in range(EPILOGUE_UNROLL):
            r0 = r_base + u * rc
            x = x_ref[pl.ds(r0, rc), :] + _rows(gate_ref, r0, rc) * acc_ref[pl.ds(r0, rc), :]
            xo_ref[pl.ds(r0, rc), :] = x
            ho_ref[pl.ds(r0, rc), :] = _adaln(x, ng_ref[...], _rows(sc_ref, r0, rc),
                                              _rows(sh_ref, r0, rc)).astype(BF16)
    _for_chunks(tm, EPILOGUE_UNROLL * rc, body)


def _merge_kernel(l_ref, h_ref, ya_ref, df_ref, z_ref, x_ref, g1_ref, sc2_ref, sh2_ref, ng_ref,
                  woa_ref, pw_ref, ps_ref, wp2_ref, bp2_ref, wg0_ref, wg1_ref, wg2_ref, wo_ref,
                  xo_ref, ho_ref, acc_ref, *, tm, nj):
    j = pl.program_id(1)

    @pl.when(j == 0)
    def _():
        acc_ref[...] = jnp.zeros_like(acc_ref)

    h = h_ref[...]
    y_a = _dot(ya_ref[...], woa_ref[...])
    y_b = _dot(df_ref[...], pw_ref[...]) * ps_ref[...]
    y_c = _dot(z_ref[...], wp2_ref[...]) + bp2_ref[...]
    merged = (jax.nn.sigmoid(_dot(h, wg0_ref[...])) * y_a
              + jax.nn.sigmoid(_dot(h, wg1_ref[...])) * y_b
              + jax.nn.sigmoid(_dot(h, wg2_ref[...])) * y_c)
    acc_ref[...] += _dot(merged.astype(BF16), wo_ref[...])

    @pl.when(j == nj - 1)
    def _():
        _residual_epilogue(x_ref, g1_ref, acc_ref, ng_ref, sc2_ref, sh2_ref, xo_ref, ho_ref, tm)


def _merge(grp, l_arr, h, ya, diffs, z, x, mod, norm_g, w_in, w_out_a, pool_w, pool_scale,
           w_pw2, b_pw2, w_o):
    tm, tn = grp.tm, MERGE_COL_TILE
    nj = D_MODEL // tn
    per_pool = POOL_OUT // tn
    off_g = (3 * D_A + D_B + 2 * D_C) // tn
    per_g = D_MODEL // tn
    row = lambda c: pl.BlockSpec((tm, c), lambda i, j, l: (i, 0))
    in_specs = [
        row(D_MODEL), row(D_A),
        pl.BlockSpec((tm, POOL_IN), lambda i, j, l: (i, j // per_pool)),
        row(D_C), row(D_MODEL),
        grp.mod_spec(G1, 0), grp.mod_spec(SC2, 0), grp.mod_spec(SH2, 0),
        pl.BlockSpec((None, 1, D_MODEL), lambda i, j, l: (l[0], 0, 0)),
        pl.BlockSpec((None, D_A, tn), lambda i, j, l: (l[0], 0, j)),
        pl.BlockSpec((None, None, POOL_IN, tn), lambda i, j, l: (l[0], j // per_pool, 0, j % per_pool)),
        pl.BlockSpec((None, 1, tn), lambda i, j, l: (l[0], 0, j)),
        pl.BlockSpec((None, D_C, tn), lambda i, j, l: (l[0], 0, j)),
        pl.BlockSpec((None, 1, tn), lambda i, j, l: (l[0], 0, j)),
        pl.BlockSpec((None, D_MODEL, tn), lambda i, j, l: (l[0], 0, off_g + j)),
        pl.BlockSpec((None, D_MODEL, tn), lambda i, j, l: (l[0], 0, off_g + per_g + j)),
        pl.BlockSpec((None, D_MODEL, tn), lambda i, j, l: (l[0], 0, off_g + 2 * per_g + j)),
        pl.BlockSpec((None, tn, D_MODEL), lambda i, j, l: (l[0], j, 0)),
    ]
    act_bytes = tm * (D_MODEL * 2 + D_A * 2 + POOL_IN * 2 + D_C * 2 + D_MODEL * 4)
    w_bytes = (D_A + POOL_IN + D_C + 3 * D_MODEL + D_MODEL) * tn * 2
    out_bytes = tm * D_MODEL * (4 + 2)
    vmem = 2 * (act_bytes + w_bytes + out_bytes) + tm * D_MODEL * 4 + 16 * tm * tn * 4 + (4 << 20)
    args = [h, ya, diffs, z, x, mod, mod, mod, norm_g, w_out_a, pool_w, pool_scale, w_pw2, b_pw2,
            w_in, w_in, w_in, w_o]
    return _call(functools.partial(_merge_kernel, tm=tm, nj=nj), grp.name, "merge",
                 (grp.n_tiles, nj), in_specs, [row(D_MODEL), row(D_MODEL)],
                 [jax.ShapeDtypeStruct((grp.rows, D_MODEL), F32),
                  jax.ShapeDtypeStruct((grp.rows, D_MODEL), BF16)],
                 [pltpu.VMEM((tm, D_MODEL), F32)], vmem, l_arr, args)


def _ffn_kernel(l_ref, h_ref, x_ref, g2_ref, wg_ref, wu_ref, wd_ref, ng_ref, sc_ref, sh_ref,
                xo_ref, ho_ref, acc_ref, *, tm, nk):
    k = pl.program_id(1)

    @pl.when(k == 0)
    def _():
        acc_ref[...] = jnp.zeros_like(acc_ref)

    h = h_ref[...]
    tn = wg_ref.shape[1]
    part = None
    for c0 in range(0, tn, FFN_SUB_TILE):
        cols = slice(c0, c0 + FFN_SUB_TILE)
        gate = _dot(h, wg_ref[:, cols])
        act = (gate * jax.nn.sigmoid(gate)) * _dot(h, wu_ref[:, cols])
        p = _dot(act.astype(BF16), wd_ref[cols, :])
        part = p if part is None else part + p
    acc_ref[...] += part

    @pl.when(k == nk - 1)
    def _():
        _residual_epilogue(x_ref, g2_ref, acc_ref, ng_ref, sc_ref, sh_ref, xo_ref, ho_ref, tm)


def _ffn(grp, l_arr, h, x, mod, w_gate_up, w_down, norm_mix_g):
    tm, tn = grp.tm, COL_TILE
    nk = D_FF // tn
    row = pl.BlockSpec((tm, D_MODEL), lambda i, k, l: (i, 0))
    in_specs = [
        row, row, grp.mod_spec(G2, 0),
        pl.BlockSpec((None, D_MODEL, tn), lambda i, k, l: (l[0], 0, k)),
        pl.BlockSpec((None, D_MODEL, tn), lambda i, k, l: (l[0], 0, nk + k)),
        pl.BlockSpec((None, tn, D_MODEL), lambda i, k, l: (l[0], k, 0)),
        pl.BlockSpec((None, 1, D_MODEL), lambda i, k, l: (jnp.minimum(l[0] + 1, DEPTH - 1), 0, 0)),
        grp.next_mod_spec(SC1, 0), grp.next_mod_spec(SH1, 0),
    ]
    args = [h, x, mod, w_gate_up, w_gate_up, w_down, norm_mix_g, mod, mod]
    vmem = (2 * (tm * D_MODEL * (2 + 4) + 3 * D_MODEL * tn * 2 + tm * D_MODEL * (4 + 2))
            + tm * D_MODEL * 4 + 10 * tm * tn * 4 + (4 << 20))
    return _call(functools.partial(_ffn_kernel, tm=tm, nk=nk), grp.name, "ffn",
                 (grp.n_tiles, nk), in_specs, [row, row],
                 [jax.ShapeDtypeStruct((grp.rows, D_MODEL), F32),
                  jax.ShapeDtypeStruct((grp.rows, D_MODEL), BF16)],
                 [pltpu.VMEM((tm, D_MODEL), F32)], vmem, l_arr, args)


def _layer(grp, l_arr, x, h, mod, states, p):
    st_a, st_p, st_c = states if states is not None else (None, None, None)
    ya, tail_a = _branch_a(grp, l_arr, h, p["w_in"], p["conv_a_w"], st_a)
    diffs, tail_p = _branch_b(grp, l_arr, h, p["w_in"], st_p)
    z, tail_c = _branch_c(grp, l_arr, h, p["w_in"], p["conv_c_w"], p["conv_c_b"], p["ln_c_g"],
                          p["ln_c_b"], st_c)
    x, h = _merge(grp, l_arr, h, ya, diffs, z, x, mod, p["norm_ffn_g"], p["w_in"], p["w_out_a"],
                  p["pool_w"], p["pool_scale"], p["w_pw2"], p["b_pw2"], p["w_o"])
    x, h = _ffn(grp, l_arr, h, x, mod, p["w_gate_up"], p["w_down"], p["norm_mix_g"])
    return x, h, (tail_a, tail_p, tail_c)


def kernel(x_prompt, x_sample, c_prompt, c_sample, state_conv_a, state_pool, state_conv_c,
           ada_w, ada_b, norm_mix_g, w_in, conv_a_w, w_out_a, pool_w, pool_scale,
           conv_c_w, conv_c_b, ln_c_g, ln_c_b, w_pw2, b_pw2, w_o, norm_ffn_g,
           w_gate_up, w_down, final_norm_g):
    vec3 = lambda a: a.reshape(DEPTH, 1, a.shape[-1])
    p = {
        "w_in": w_in.astype(BF16), "w_out_a": w_out_a.astype(BF16), "pool_w": pool_w.astype(BF16),
        "w_pw2": w_pw2.astype(BF16), "w_o": w_o.astype(BF16), "w_gate_up": w_gate_up.astype(BF16),
        "w_down": w_down.astype(BF16),
        "conv_a_w": conv_a_w, "conv_c_w": conv_c_w,
        "conv_c_b": vec3(conv_c_b), "ln_c_g": vec3(ln_c_g), "ln_c_b": vec3(ln_c_b),
        "pool_scale": vec3(pool_scale), "b_pw2": vec3(b_pw2), "norm_ffn_g": vec3(norm_ffn_g),
        "norm_mix_g": vec3(norm_mix_g),
    }

    c_all = jnp.concatenate(
        [c_sample, c_prompt, jnp.zeros((C_ALL_ROWS - DEC_BATCH - BATCH, D_MODEL), F32)], axis=0)
    mod_s = _modulation(c_all, ada_w, ada_b)
    mod_p = mod_s[:, DEC_BATCH:DEC_BATCH + BATCH].reshape(DEPTH, BATCH, 1, 6 * D_MODEL)

    def time_major(s):
        s = jnp.swapaxes(s, -3, -2)
        return s.reshape(s.shape[:-3] + (-1, s.shape[-1]))

    states_s = (time_major(state_conv_a), time_major(state_pool), time_major(state_conv_c))
    x_p = x_prompt.reshape(BATCH * SEQ, D_MODEL)
    x_s = time_major(x_sample)

    l0 = jnp.zeros((1,), jnp.int32)
    h_p = _norm(PROMPT, l0, x_p, norm_mix_g[0:1], mod_p)
    h_s = _norm(SAMPLE, l0, x_s, norm_mix_g[0:1], mod_s)

    def body(carry, l):
        x_p, h_p, x_s, h_s = carry
        l_arr = jnp.reshape(l, (1,)).astype(jnp.int32)
        x_p, h_p, tails_p = _layer(PROMPT, l_arr, x_p, h_p, mod_p, None, p)
        x_s, h_s, tails_s = _layer(SAMPLE, l_arr, x_s, h_s, mod_s, states_s, p)
        return (x_p, h_p, x_s, h_s), (tails_p, tails_s)

    (x_p, _, x_s, _), (tails_p, tails_s) = lax.scan(body, (x_p, h_p, x_s, h_s), jnp.arange(DEPTH))
    y_p = _norm(PROMPT, l0, x_p, final_norm_g.reshape(1, D_MODEL), None)
    y_s = _norm(SAMPLE, l0, x_s, final_norm_g.reshape(1, D_MODEL), None)

    def from_time_major(a):
        a = a.reshape(a.shape[:-2] + (DEC_SEQ, DEC_BATCH, a.shape[-1]))
        return jnp.swapaxes(a, -3, -2)

    hist = (CONV_A - 1, POOL_BUF, CONV_C - 1)
    old_s = (state_conv_a, state_pool, state_conv_c)
    new_p = [tails_p[b][:, :, -hist[b]:, :] for b in range(3)]
    new_s = [jnp.concatenate([old_s[b], from_time_major(tails_s[b])], axis=2)[:, :, -hist[b]:, :]
             for b in range(3)]
    return (y_p.reshape(BATCH, SEQ, D_MODEL), from_time_major(y_s),
            new_p[0], new_p[1], new_p[2], new_s[0], new_s[1], new_s[2])
```

```python
import functools

import jax
import jax.numpy as jnp
from jax import lax
from jax.experimental import pallas as pl
from jax.experimental.pallas import tpu as pltpu

D_MODEL = 2048
BATCH = 4
SEQ = 2048
DEPTH = 4
DEC_BATCH = 128
DEC_SEQ = 4
PAST_LEN = 16384

D_A = D_MODEL // 2
CONV_A = 3
D_B = D_MODEL // 2
POOL_WINDOWS = (2, 4, 8, 16)
N_POOL = len(POOL_WINDOWS)
POOL_IN = D_B // N_POOL
POOL_OUT = D_MODEL // N_POOL
POOL_BUF = max(POOL_WINDOWS) - 1
D_C = D_MODEL // 2
CONV_C = 31
D_FF = 5632
EPS = 1e-6

F32 = jnp.float32
BF16 = jnp.bfloat16

V7X_SUBLANES = 8
V7X_LANES = 128
V7X_VREG_ELEMS = V7X_SUBLANES * V7X_LANES
V7X_VMEM_BYTES = 64 * 1024 * 1024
V7X_VMEM_USABLE_BYTES = V7X_VMEM_BYTES - 6 * 1024 * 1024

ROW_TILE = 512
COL_TILE = 512
C_ALL_ROWS = 144
MOD_COL_TILE = 1024
ACC_VREGS = 32
EPILOGUE_UNROLL = 4
FFN_SUB_TILE = 256


def _round_up(n, m):
    return -(-n // m) * m


def _chunk_rows(width):
    return ACC_VREGS * V7X_VREG_ELEMS // width


def _params(vmem_bytes, n_axes):
    limit = min(V7X_VMEM_USABLE_BYTES, int(vmem_bytes))
    return pltpu.CompilerParams(dimension_semantics=("arbitrary",) * n_axes, vmem_limit_bytes=limit)


def _dot(a, b):
    return jnp.dot(a, b, preferred_element_type=F32)


def _rows(m, r0, rc):
    if m.shape[0] == 1:
        return m[...]
    n_seq = m.shape[0]
    if rc <= n_seq:
        return m[pl.ds(r0 % n_seq if isinstance(r0, int) else lax.rem(r0, n_seq), rc), :]
    raise NotImplementedError


def _adaln(x, gain, scale, shift):
    y = x * lax.rsqrt(jnp.mean(x * x, axis=-1, keepdims=True) + EPS)
    return (y * gain) * (1.0 + scale) + shift


def _for_chunks(tm, rc, body):
    def step(c, carry):
        body(pl.multiple_of(c * rc, rc))
        return carry
    lax.fori_loop(0, tm // rc, step, 0)


class Group:
    def __init__(self, name, n_seq, t_len, time_major, p0):
        self.name = name
        self.n_seq = n_seq
        self.t_len = t_len
        self.time_major = time_major
        self.p0 = p0
        self.rows = n_seq * t_len
        self.tm = ROW_TILE
        self.n_tiles = self.rows // self.tm
        self.stride = n_seq if time_major else 1
        self.tps = 1 if time_major else t_len // self.tm
        assert self.rows % self.tm == 0
        assert (time_major and self.n_tiles == 1) or (not time_major and t_len % self.tm == 0)

    def hist_rows(self, k_hist):
        return _round_up(k_hist * self.stride, V7X_SUBLANES)

    def tail_rows(self, k_hist):
        return self.tm if self.time_major else _round_up(k_hist, V7X_SUBLANES)

    def mod_spec(self, which, row_axis):
        if self.time_major:
            return pl.BlockSpec((None, self.n_seq, D_MODEL), lambda *g: (g[-1][0], 0, which))
        tps = self.tps
        return pl.BlockSpec((None, None, 1, D_MODEL),
                            lambda *g: (g[-1][0], g[row_axis] // tps, 0, which))

    def next_mod_spec(self, which, row_axis):
        nxt = lambda l_ref: jnp.minimum(l_ref[0] + 1, DEPTH - 1)
        if self.time_major:
            return pl.BlockSpec((None, self.n_seq, D_MODEL), lambda *g: (nxt(g[-1]), 0, which))
        tps = self.tps
        return pl.BlockSpec((None, None, 1, D_MODEL),
                            lambda *g: (nxt(g[-1]), g[row_axis] // tps, 0, which))


PROMPT = Group("prompt", BATCH, SEQ, False, 0)
SAMPLE = Group("sample", DEC_BATCH, DEC_SEQ, True, PAST_LEN)
SH1, SC1, G1, SH2, SC2, G2 = range(6)


def _call(kernel, grp_name, name, grid, in_specs, out_specs, out_shape, scratch, vmem, l_arr, args):
    grid_spec = pltpu.PrefetchScalarGridSpec(
        num_scalar_prefetch=1, grid=grid, in_specs=in_specs, out_specs=out_specs,
        scratch_shapes=scratch)
    return pl.pallas_call(kernel, grid_spec=grid_spec, out_shape=out_shape,
                          compiler_params=_params(vmem, len(grid)),
                          name=f"{name}_{grp_name}")(l_arr, *args)


def _mod_kernel(c_ref, w_ref, b_ref, o_ref):
    c = c_ref[...]
    a = (c * jax.nn.sigmoid(c)).astype(BF16)
    o_ref[...] = _dot(a, w_ref[...].astype(BF16)) + b_ref[...]


def _modulation(c_all, ada_w, ada_b):
    n_cols = 6 * D_MODEL
    tn = MOD_COL_TILE
    vmem = 2 * (D_MODEL * tn * 4) + D_MODEL * tn * 2 + 4 * C_ALL_ROWS * (D_MODEL + 2 * tn) * 4 + (4 << 20)
    return pl.pallas_call(
        _mod_kernel,
        grid=(DEPTH, n_cols // tn),
        in_specs=[
            pl.BlockSpec((C_ALL_ROWS, D_MODEL), lambda l, n: (0, 0)),
            pl.BlockSpec((None, D_MODEL, tn), lambda l, n: (l, 0, n)),
            pl.BlockSpec((None, 1, tn), lambda l, n: (l, 0, n)),
        ],
        out_specs=pl.BlockSpec((None, C_ALL_ROWS, tn), lambda l, n: (l, 0, n)),
        out_shape=jax.ShapeDtypeStruct((DEPTH, C_ALL_ROWS, n_cols), F32),
        compiler_params=_params(vmem, 2),
        name="modulation",
    )(c_all, ada_w, ada_b.reshape(DEPTH, 1, n_cols))


def _norm_kernel(l_ref, x_ref, gain_ref, *rest, tm, modulated):
    rc = _chunk_rows(D_MODEL)
    if modulated:
        sc_ref, sh_ref, o_ref = rest
    else:
        (o_ref,) = rest

    def body(r0):
        x = x_ref[pl.ds(r0, rc), :]
        if modulated:
            o_ref[pl.ds(r0, rc), :] = _adaln(x, gain_ref[...], _rows(sc_ref, r0, rc),
                                             _rows(sh_ref, r0, rc)).astype(BF16)
        else:
            o_ref[pl.ds(r0, rc), :] = (
                x * lax.rsqrt(jnp.mean(x * x, axis=-1, keepdims=True) + EPS) * gain_ref[...])
    _for_chunks(tm, rc, body)


def _norm(grp, l_arr, x, gain, mod):
    tm = grp.tm
    modulated = mod is not None
    row = pl.BlockSpec((tm, D_MODEL), lambda i, l: (i, 0))
    in_specs = [row, pl.BlockSpec((1, D_MODEL), lambda i, l: (0, 0))]
    args = [x, gain]
    if modulated:
        in_specs += [grp.mod_spec(SC1, 0), grp.mod_spec(SH1, 0)]
        args += [mod, mod]
    return _call(functools.partial(_norm_kernel, tm=tm, modulated=modulated), grp.name,
                 "norm_mod" if modulated else "norm_out", (grp.n_tiles,), in_specs, row,
                 jax.ShapeDtypeStruct((grp.rows, D_MODEL), BF16 if modulated else F32), [],
                 6 * tm * D_MODEL * 4 + (8 << 20), l_arr, args)


def _hist_begin(xx_ref, hp, tm, t_in_seq):
    @pl.when(t_in_seq == 0)
    def _():
        xx_ref[pl.ds(0, hp), :] = jnp.zeros((hp, xx_ref.shape[1]), F32)

    @pl.when(t_in_seq != 0)
    def _():
        xx_ref[pl.ds(0, hp), :] = xx_ref[pl.ds(tm, hp), :]


def _tap_rowmajor(xx_ref, hp, k_hist, r0, rc, cols=slice(None)):
    return lambda j: xx_ref[pl.ds(r0 + hp - (k_hist - j), rc), cols]


def _tap_timemajor(st_ref, new_ref, n_seq, k_hist, t, cols=slice(None)):
    def tap(j):
        step = t + j
        if step < k_hist:
            return st_ref[pl.ds(step * n_seq, n_seq), cols]
        return new_ref[pl.ds((step - k_hist) * n_seq, n_seq), cols]
    return tap


def _dwconv(tap, w_ref, k_w):
    acc = None
    for k in range(k_w):
        term = tap(k) * w_ref[k:k + 1, :]
        acc = term if acc is None else acc + term
    return acc


def _tail_store(grp, tail_ref, xx_ref, hp, k_hist, t_in_seq):
    r = grp.tail_rows(k_hist)

    @pl.when(t_in_seq == grp.tps - 1)
    def _():
        tail_ref[...] = xx_ref[pl.ds(hp + grp.tm - r, r), :]


def _tile_chunks(grp, width):
    if grp.time_major:
        assert grp.n_seq * width <= 4 * ACC_VREGS * V7X_VREG_ELEMS
        return [(t * grp.n_seq, grp.n_seq, t) for t in range(grp.t_len)]
    rc = _chunk_rows(width)
    return [(r0, rc, None) for r0 in range(0, grp.tm, rc)]


def _branch_a_kernel(l_ref, *refs, grp, tc):
    if grp.time_major:
        h_ref, wb_ref, wc_ref, wv_ref, cw_ref, st_ref, ya_ref, new_ref, bg_ref = refs
    else:
        h_ref, wb_ref, wc_ref, wv_ref, cw_ref, ya_ref, tail_ref, xx_ref, bg_ref = refs
    tm, k_hist = grp.tm, CONV_A - 1
    h = h_ref[...]
    bg_ref[...] = _dot(h, wb_ref[...])
    cv = _dot(h, wc_ref[...]) * _dot(h, wv_ref[...])
    if grp.time_major:
        new_ref[...] = cv
    else:
        t_in_seq = pl.program_id(1) % grp.tps
        hp = grp.hist_rows(k_hist)
        _hist_begin(xx_ref, hp, tm, t_in_seq)
        xx_ref[pl.ds(hp, tm), :] = cv
        _tail_store(grp, tail_ref, xx_ref, hp, k_hist, t_in_seq)
    for r0, rc, t in _tile_chunks(grp, tc):
        if grp.time_major:
            tap = _tap_timemajor(st_ref, new_ref, grp.n_seq, k_hist, t)
        else:
            tap = _tap_rowmajor(xx_ref, hp, k_hist, r0, rc)
        ya_ref[pl.ds(r0, rc), :] = (bg_ref[pl.ds(r0, rc), :] * _dwconv(tap, cw_ref, CONV_A)).astype(BF16)


def _branch_a(grp, l_arr, h, w_in, conv_w, state):
    tm, tc, k_hist = grp.tm, COL_TILE, CONV_A - 1
    nq = D_A // tc
    off_c, off_v = D_A // tc, 2 * D_A // tc
    in_specs = [
        pl.BlockSpec((tm, D_MODEL), lambda q, i, l: (i, 0)),
        pl.BlockSpec((None, D_MODEL, tc), lambda q, i, l: (l[0], 0, q)),
        pl.BlockSpec((None, D_MODEL, tc), lambda q, i, l: (l[0], 0, off_c + q)),
        pl.BlockSpec((None, D_MODEL, tc), lambda q, i, l: (l[0], 0, off_v + q)),
        pl.BlockSpec((None, CONV_A, tc), lambda q, i, l: (l[0], 0, q)),
    ]
    args = [h, w_in, w_in, w_in, conv_w]
    scratch = []
    if grp.time_major:
        in_specs.append(pl.BlockSpec((None, k_hist * grp.n_seq, tc), lambda q, i, l: (l[0], 0, q)))
        args.append(state)
        tail_spec = pl.BlockSpec((tm, tc), lambda q, i, l: (0, q))
        tail_shape = jax.ShapeDtypeStruct((tm, D_A), F32)
    else:
        tps = grp.tps
        r = grp.tail_rows(k_hist)
        scratch.append(pltpu.VMEM((grp.hist_rows(k_hist) + tm, tc), F32))
        tail_spec = pl.BlockSpec((None, r, tc), lambda q, i, l: (i // tps, 0, q))
        tail_shape = jax.ShapeDtypeStruct((grp.n_seq, r, D_A), F32)
    scratch.append(pltpu.VMEM((tm, tc), F32))
    vmem = 2 * (tm * D_MODEL * 2 + 3 * D_MODEL * tc * 2 + tm * tc * 2 + tm * tc * 4) + 10 * tm * tc * 4 + (6 << 20)
    return _call(functools.partial(_branch_a_kernel, grp=grp, tc=tc), grp.name, "branch_a",
                 (nq, grp.n_tiles), in_specs,
                 [pl.BlockSpec((tm, tc), lambda q, i, l: (i, q)), tail_spec],
                 [jax.ShapeDtypeStruct((grp.rows, D_A), BF16), tail_shape], scratch, vmem, l_arr, args)


def _branch_b_kernel(l_ref, *refs, grp):
    if grp.time_major:
        h_ref, w_ref, st_ref, diff_ref, new_ref = refs
    else:
        h_ref, w_ref, diff_ref, tail_ref, xx_ref = refs
    tm, k_hist = grp.tm, POOL_BUF
    u = _dot(h_ref[...], w_ref[...])
    if grp.time_major:
        new_ref[...] = u
    else:
        t_in_seq = pl.program_id(0) % grp.tps
        hp = grp.hist_rows(k_hist)
        _hist_begin(xx_ref, hp, tm, t_in_seq)
        xx_ref[pl.ds(hp, tm), :] = u
        _tail_store(grp, tail_ref, xx_ref, hp, k_hist, t_in_seq)
    for g, win in enumerate(POOL_WINDOWS):
        cols = slice(g * POOL_IN, (g + 1) * POOL_IN)
        for r0, rc, t in _tile_chunks(grp, POOL_IN):
            if grp.time_major:
                tap = _tap_timemajor(st_ref, new_ref, grp.n_seq, k_hist, t, cols)
                assert grp.p0 + 1 >= win
                cnt = float(win)
            else:
                tap = _tap_rowmajor(xx_ref, hp, k_hist, r0, rc, cols)
                pos = grp.p0 + t_in_seq * tm + r0 + lax.broadcasted_iota(jnp.int32, (rc, 1), 0)
                cnt = jnp.minimum(pos + 1, win).astype(F32)
            cur = tap(k_hist)
            total = cur
            for j in range(1, win):
                total = total + tap(k_hist - j)
            diff_ref[pl.ds(r0, rc), cols] = (total / cnt - cur).astype(BF16)


def _branch_b(grp, l_arr, h, w_in, state):
    tm, k_hist = grp.tm, POOL_BUF
    off = 3 * D_A // D_B
    in_specs = [
        pl.BlockSpec((tm, D_MODEL), lambda i, l: (i, 0)),
        pl.BlockSpec((None, D_MODEL, D_B), lambda i, l: (l[0], 0, off)),
    ]
    args = [h, w_in]
    scratch = []
    if grp.time_major:
        in_specs.append(pl.BlockSpec((None, k_hist * grp.n_seq, D_B), lambda i, l: (l[0], 0, 0)))
        args.append(state)
        tail_spec = pl.BlockSpec((tm, D_B), lambda i, l: (0, 0))
        tail_shape = jax.ShapeDtypeStruct((tm, D_B), F32)
        state_bytes = 2 * k_hist * grp.n_seq * D_B * 4
    else:
        tps = grp.tps
        r = grp.tail_rows(k_hist)
        scratch.append(pltpu.VMEM((grp.hist_rows(k_hist) + tm, D_B), F32))
        tail_spec = pl.BlockSpec((None, r, D_B), lambda i, l: (i // tps, 0, 0))
        tail_shape = jax.ShapeDtypeStruct((grp.n_seq, r, D_B), F32)
        state_bytes = 0
    vmem = (2 * (tm * D_MODEL * 2 + D_MODEL * D_B * 2 + tm * D_B * 2 + tm * D_B * 4) + 8 * tm * D_B * 4
            + state_bytes + (6 << 20))
    return _call(functools.partial(_branch_b_kernel, grp=grp), grp.name, "branch_b",
                 (grp.n_tiles,), in_specs,
                 [pl.BlockSpec((tm, D_B), lambda i, l: (i, 0)), tail_spec],
                 [jax.ShapeDtypeStruct((grp.rows, D_B), BF16), tail_shape], scratch, vmem, l_arr, args)


def _branch_c_kernel(l_ref, *refs, grp):
    if grp.time_major:
        h_ref, w1_ref, w2_ref, cw_ref, cb_ref, lg_ref, lb_ref, st_ref, z_ref, new_ref = refs
    else:
        h_ref, w1_ref, w2_ref, cw_ref, cb_ref, lg_ref, lb_ref, z_ref, tail_ref, xs_ref = refs
    tm, k_hist = grp.tm, CONV_C - 1
    h = h_ref[...]
    glu = _dot(h, w1_ref[...]) * jax.nn.sigmoid(_dot(h, w2_ref[...]))
    if grp.time_major:
        new_ref[...] = glu
    else:
        t_in_seq = pl.program_id(0) % grp.tps
        hp = grp.hist_rows(k_hist)
        x0 = xs_ref.at[0]
        _hist_begin(x0, hp, tm, t_in_seq)
        x0[pl.ds(hp, tm), :] = glu
        x0[pl.ds(hp + tm, V7X_SUBLANES), :] = jnp.zeros((V7X_SUBLANES, D_C), F32)
        _tail_store(grp, tail_ref, x0, hp, k_hist, t_in_seq)
        rc_copy = _chunk_rows(D_C)
        for b in range(1, V7X_SUBLANES):
            for r0 in range(0, hp + tm, rc_copy):
                xs_ref[b, pl.ds(r0, rc_copy), :] = x0[pl.ds(r0 + b, rc_copy), :]
    for r0, rc, t in _tile_chunks(grp, D_C):
        if grp.time_major:
            tap = _tap_timemajor(st_ref, new_ref, grp.n_seq, k_hist, t)
        else:
            def tap(k, r0=r0, rc=rc):
                a, b = divmod(hp - k_hist + k, V7X_SUBLANES)
                return xs_ref[b, pl.ds(r0 + V7X_SUBLANES * a, rc), :]
        v = _dwconv(tap, cw_ref, CONV_C) + cb_ref[...]
        mu = jnp.mean(v, axis=-1, keepdims=True)
        vc = v - mu
        var = jnp.mean(vc * vc, axis=-1, keepdims=True)
        y = vc * lax.rsqrt(var + EPS) * lg_ref[...] + lb_ref[...]
        z_ref[pl.ds(r0, rc), :] = (y * jax.nn.sigmoid(y)).astype(BF16)


def _branch_c(grp, l_arr, h, w_in, conv_w, conv_b, ln_g, ln_b, state):
    tm, k_hist = grp.tm, CONV_C - 1
    off1 = (3 * D_A + D_B) // D_C
    vec = pl.BlockSpec((None, 1, D_C), lambda i, l: (l[0], 0, 0))
    in_specs = [
        pl.BlockSpec((tm, D_MODEL), lambda i, l: (i, 0)),
        pl.BlockSpec((None, D_MODEL, D_C), lambda i, l: (l[0], 0, off1)),
        pl.BlockSpec((None, D_MODEL, D_C), lambda i, l: (l[0], 0, off1 + 1)),
        pl.BlockSpec((None, CONV_C, D_C), lambda i, l: (l[0], 0, 0)),
        vec, vec, vec,
    ]
    args = [h, w_in, w_in, conv_w, conv_b, ln_g, ln_b]
    scratch = []
    if grp.time_major:
        in_specs.append(pl.BlockSpec((None, k_hist * grp.n_seq, D_C), lambda i, l: (l[0], 0, 0),
                                     pipeline_mode=pl.Buffered(1)))
        args.append(state)
        tail_spec = pl.BlockSpec((tm, D_C), lambda i, l: (0, 0))
        tail_shape = jax.ShapeDtypeStruct((tm, D_C), F32)
        state_bytes = k_hist * grp.n_seq * D_C * 4
    else:
        tps = grp.tps
        r = grp.tail_rows(k_hist)
        shifted_rows = grp.hist_rows(k_hist) + tm + V7X_SUBLANES
        scratch.append(pltpu.VMEM((V7X_SUBLANES, shifted_rows, D_C), F32))
        tail_spec = pl.BlockSpec((None, r, D_C), lambda i, l: (i // tps, 0, 0))
        tail_shape = jax.ShapeDtypeStruct((grp.n_seq, r, D_C), F32)
        state_bytes = V7X_SUBLANES * shifted_rows * D_C * 4
    vmem = (2 * (tm * D_MODEL * 2 + 2 * D_MODEL * D_C * 2 + tm * D_C * 2 + tm * D_C * 4)
            + 10 * tm * D_C * 4 + state_bytes + (6 << 20))
    return _call(functools.partial(_branch_c_kernel, grp=grp), grp.name, "branch_c",
                 (grp.n_tiles,), in_specs,
                 [pl.BlockSpec((tm, D_C), lambda i, l: (i, 0)), tail_spec],
                 [jax.ShapeDtypeStruct((grp.rows, D_C), BF16), tail_shape], scratch, vmem, l_arr, args)


def _residual_epilogue(x_ref, gate_ref, acc_ref, ng_ref, sc_ref, sh_ref, xo_ref, ho_ref, tm):
    rc = _chunk_rows(D_MODEL)

    def body(r_base):
        for u in range(EPILOGUE_UNROLL):
            r0 = r_base + u * rc
            x = x_ref[pl.ds(r0, rc), :] + _rows(gate_ref, r0, rc) * acc_ref[pl.ds(r0, rc), :]
            xo_ref[pl.ds(r0, rc), :] = x
            ho_ref[pl.ds(r0, rc), :] = _adaln(x, ng_ref[...], _rows(sc_ref, r0, rc),
                                              _rows(sh_ref, r0, rc)).astype(BF16)
    _for_chunks(tm, EPILOGUE_UNROLL * rc, body)


def _merge_kernel(l_ref, h_ref, ya_ref, df_ref, z_ref, woa_ref, pw_ref, ps_ref, wp2_ref, bp2_ref,
                  wg0_ref, wg1_ref, wg2_ref, m_ref):
    h = h_ref[...]
    y_a = _dot(ya_ref[...], woa_ref[...])
    y_b = _dot(df_ref[...], pw_ref[...]) * ps_ref[...]
    y_c = _dot(z_ref[...], wp2_ref[...]) + bp2_ref[...]
    merged = (jax.nn.sigmoid(_dot(h, wg0_ref[...])) * y_a
              + jax.nn.sigmoid(_dot(h, wg1_ref[...])) * y_b
              + jax.nn.sigmoid(_dot(h, wg2_ref[...])) * y_c)
    m_ref[...] = merged.astype(BF16)


def _merge(grp, l_arr, h, ya, diffs, z, w_in, w_out_a, pool_w, pool_scale, w_pw2, b_pw2):
    tm, tn = grp.tm, POOL_OUT
    nj = D_MODEL // tn
    off_g = (3 * D_A + D_B + 2 * D_C) // tn
    per_g = D_MODEL // tn
    row = lambda c: pl.BlockSpec((tm, c), lambda j, i, l: (i, 0))
    in_specs = [
        row(D_MODEL), row(D_A),
        pl.BlockSpec((tm, POOL_IN), lambda j, i, l: (i, j)),
        row(D_C),
        pl.BlockSpec((None, D_A, tn), lambda j, i, l: (l[0], 0, j)),
        pl.BlockSpec((None, None, POOL_IN, tn), lambda j, i, l: (l[0], j, 0, 0)),
        pl.BlockSpec((None, 1, tn), lambda j, i, l: (l[0], 0, j)),
        pl.BlockSpec((None, D_C, tn), lambda j, i, l: (l[0], 0, j)),
        pl.BlockSpec((None, 1, tn), lambda j, i, l: (l[0], 0, j)),
        pl.BlockSpec((None, D_MODEL, tn), lambda j, i, l: (l[0], 0, off_g + j)),
        pl.BlockSpec((None, D_MODEL, tn), lambda j, i, l: (l[0], 0, off_g + per_g + j)),
        pl.BlockSpec((None, D_MODEL, tn), lambda j, i, l: (l[0], 0, off_g + 2 * per_g + j)),
    ]
    act_bytes = tm * (D_MODEL + D_A + POOL_IN + D_C + tn) * 2
    w_bytes = (D_A + POOL_IN + D_C + 3 * D_MODEL) * tn * 2
    vmem = 2 * (act_bytes + w_bytes) + 16 * tm * tn * 4 + (4 << 20)
    args = [h, ya, diffs, z, w_out_a, pool_w, pool_scale, w_pw2, b_pw2, w_in, w_in, w_in]
    return _call(_merge_kernel, grp.name, "merge", (nj, grp.n_tiles), in_specs,
                 pl.BlockSpec((tm, tn), lambda j, i, l: (i, j)),
                 jax.ShapeDtypeStruct((grp.rows, D_MODEL), BF16), [], vmem, l_arr, args)


def _out_proj_kernel(l_ref, m_ref, x_ref, g1_ref, sc2_ref, sh2_ref, ng_ref, wo_ref,
                     xo_ref, ho_ref, acc_ref, *, tm):
    acc_ref[...] = _dot(m_ref[...], wo_ref[...])
    _residual_epilogue(x_ref, g1_ref, acc_ref, ng_ref, sc2_ref, sh2_ref, xo_ref, ho_ref, tm)


def _out_proj(grp, l_arr, merged, x, mod, norm_g, w_o):
    tm = grp.tm
    row = pl.BlockSpec((tm, D_MODEL), lambda i, l: (i, 0))
    in_specs = [
        row, row, grp.mod_spec(G1, 0), grp.mod_spec(SC2, 0), grp.mod_spec(SH2, 0),
        pl.BlockSpec((None, 1, D_MODEL), lambda i, l: (l[0], 0, 0)),
        pl.BlockSpec((None, D_MODEL, D_MODEL), lambda i, l: (l[0], 0, 0), pipeline_mode=pl.Buffered(1)),
    ]
    vmem = (2 * tm * D_MODEL * (2 + 4 + 4 + 2) + D_MODEL * D_MODEL * 2 + 3 * tm * D_MODEL * 4
            + (4 << 20))
    return _call(functools.partial(_out_proj_kernel, tm=tm), grp.name, "out_proj",
                 (grp.n_tiles,), in_specs, [row, row],
                 [jax.ShapeDtypeStruct((grp.rows, D_MODEL), F32),
                  jax.ShapeDtypeStruct((grp.rows, D_MODEL), BF16)],
                 [pltpu.VMEM((tm, D_MODEL), F32)], vmem, l_arr, [merged, x, mod, mod, mod, norm_g, w_o])


def _ffn_kernel(l_ref, h_ref, x_ref, g2_ref, wg_ref, wu_ref, wd_ref, ng_ref, sc_ref, sh_ref,
                xo_ref, ho_ref, acc_ref, *, tm, nk):
    k = pl.program_id(1)

    @pl.when(k == 0)
    def _():
        acc_ref[...] = jnp.zeros_like(acc_ref)

    h = h_ref[...]
    tn = wg_ref.shape[1]
    part = None
    for c0 in range(0, tn, FFN_SUB_TILE):
        cols = slice(c0, c0 + FFN_SUB_TILE)
        gate = _dot(h, wg_ref[:, cols])
        act = (gate * jax.nn.sigmoid(gate)) * _dot(h, wu_ref[:, cols])
        p = _dot(act.astype(BF16), wd_ref[cols, :])
        part = p if part is None else part + p
    acc_ref[...] += part

    @pl.when(k == nk - 1)
    def _():
        _residual_epilogue(x_ref, g2_ref, acc_ref, ng_ref, sc_ref, sh_ref, xo_ref, ho_ref, tm)


def _ffn(grp, l_arr, h, x, mod, w_gate_up, w_down, norm_mix_g):
    tm, tn = grp.tm, COL_TILE
    nk = D_FF // tn
    row = pl.BlockSpec((tm, D_MODEL), lambda i, k, l: (i, 0))
    in_specs = [
        row, row, grp.mod_spec(G2, 0),
        pl.BlockSpec((None, D_MODEL, tn), lambda i, k, l: (l[0], 0, k)),
        pl.BlockSpec((None, D_MODEL, tn), lambda i, k, l: (l[0], 0, nk + k)),
        pl.BlockSpec((None, tn, D_MODEL), lambda i, k, l: (l[0], k, 0)),
        pl.BlockSpec((None, 1, D_MODEL), lambda i, k, l: (jnp.minimum(l[0] + 1, DEPTH - 1), 0, 0)),
        grp.next_mod_spec(SC1, 0), grp.next_mod_spec(SH1, 0),
    ]
    args = [h, x, mod, w_gate_up, w_gate_up, w_down, norm_mix_g, mod, mod]
    vmem = (2 * (tm * D_MODEL * (2 + 4) + 3 * D_MODEL * tn * 2 + tm * D_MODEL * (4 + 2))
            + tm * D_MODEL * 4 + 10 * tm * tn * 4 + (4 << 20))
    return _call(functools.partial(_ffn_kernel, tm=tm, nk=nk), grp.name, "ffn",
                 (grp.n_tiles, nk), in_specs, [row, row],
                 [jax.ShapeDtypeStruct((grp.rows, D_MODEL), F32),
                  jax.ShapeDtypeStruct((grp.rows, D_MODEL), BF16)],
                 [pltpu.VMEM((tm, D_MODEL), F32)], vmem, l_arr, args)


def _layer(grp, l_arr, x, h, mod, states, p):
    st_a, st_p, st_c = states if states is not None else (None, None, None)
    ya, tail_a = _branch_a(grp, l_arr, h, p["w_in"], p["conv_a_w"], st_a)
    diffs, tail_p = _branch_b(grp, l_arr, h, p["w_in"], st_p)
    z, tail_c = _branch_c(grp, l_arr, h, p["w_in"], p["conv_c_w"], p["conv_c_b"], p["ln_c_g"],
                          p["ln_c_b"], st_c)
    merged = _merge(grp, l_arr, h, ya, diffs, z, p["w_in"], p["w_out_a"], p["pool_w"],
                    p["pool_scale"], p["w_pw2"], p["b_pw2"])
    x, h = _out_proj(grp, l_arr, merged, x, mod, p["norm_ffn_g"], p["w_o"])
    x, h = _ffn(grp, l_arr, h, x, mod, p["w_gate_up"], p["w_down"], p["norm_mix_g"])
    return x, h, (tail_a, tail_p, tail_c)


def kernel(x_prompt, x_sample, c_prompt, c_sample, state_conv_a, state_pool, state_conv_c,
           ada_w, ada_b, norm_mix_g, w_in, conv_a_w, w_out_a, pool_w, pool_scale,
           conv_c_w, conv_c_b, ln_c_g, ln_c_b, w_pw2, b_pw2, w_o, norm_ffn_g,
           w_gate_up, w_down, final_norm_g):
    vec3 = lambda a: a.reshape(DEPTH, 1, a.shape[-1])
    p = {
        "w_in": w_in.astype(BF16), "w_out_a": w_out_a.astype(BF16), "pool_w": pool_w.astype(BF16),
        "w_pw2": w_pw2.astype(BF16), "w_o": w_o.astype(BF16), "w_gate_up": w_gate_up.astype(BF16),
        "w_down": w_down.astype(BF16),
        "conv_a_w": conv_a_w, "conv_c_w": conv_c_w,
        "conv_c_b": vec3(conv_c_b), "ln_c_g": vec3(ln_c_g), "ln_c_b": vec3(ln_c_b),
        "pool_scale": vec3(pool_scale), "b_pw2": vec3(b_pw2), "norm_ffn_g": vec3(norm_ffn_g),
        "norm_mix_g": vec3(norm_mix_g),
    }

    c_all = jnp.concatenate(
        [c_sample, c_prompt, jnp.zeros((C_ALL_ROWS - DEC_BATCH - BATCH, D_MODEL), F32)], axis=0)
    mod_s = _modulation(c_all, ada_w, ada_b)
    mod_p = mod_s[:, DEC_BATCH:DEC_BATCH + BATCH].reshape(DEPTH, BATCH, 1, 6 * D_MODEL)

    def time_major(s):
        s = jnp.swapaxes(s, -3, -2)
        return s.reshape(s.shape[:-3] + (-1, s.shape[-1]))

    states_s = (time_major(state_conv_a), time_major(state_pool), time_major(state_conv_c))
    x_p = x_prompt.reshape(BATCH * SEQ, D_MODEL)
    x_s = time_major(x_sample)

    l0 = jnp.zeros((1,), jnp.int32)
    h_p = _norm(PROMPT, l0, x_p, norm_mix_g[0:1], mod_p)
    h_s = _norm(SAMPLE, l0, x_s, norm_mix_g[0:1], mod_s)

    def body(carry, l):
        x_p, h_p, x_s, h_s = carry
        l_arr = jnp.reshape(l, (1,)).astype(jnp.int32)
        x_p, h_p, tails_p = _layer(PROMPT, l_arr, x_p, h_p, mod_p, None, p)
        x_s, h_s, tails_s = _layer(SAMPLE, l_arr, x_s, h_s, mod_s, states_s, p)
        return (x_p, h_p, x_s, h_s), (tails_p, tails_s)

    (x_p, _, x_s, _), (tails_p, tails_s) = lax.scan(body, (x_p, h_p, x_s, h_s), jnp.arange(DEPTH))
    y_p = _norm(PROMPT, l0, x_p, final_norm_g.reshape(1, D_MODEL), None)
    y_s = _norm(SAMPLE, l0, x_s, final_norm_g.reshape(1, D_MODEL), None)

    def from_time_major(a):
        a = a.reshape(a.shape[:-2] + (DEC_SEQ, DEC_BATCH, a.shape[-1]))
        return jnp.swapaxes(a, -3, -2)

    hist = (CONV_A - 1, POOL_BUF, CONV_C - 1)
    old_s = (state_conv_a, state_pool, state_conv_c)
    new_p = [tails_p[b][:, :, -hist[b]:, :] for b in range(3)]
    new_s = [jnp.concatenate([old_s[b], from_time_major(tails_s[b])], axis=2)[:, :, -hist[b]:, :]
             for b in range(3)]
    return (y_p.reshape(BATCH, SEQ, D_MODEL), from_time_major(y_s),
            new_p[0], new_p[1], new_p[2], new_s[0], new_s[1], new_s[2])
```

```python
import functools

import jax
import jax.numpy as jnp
from jax import lax
from jax.experimental import pallas as pl
from jax.experimental.pallas import tpu as pltpu

D_MODEL = 2048
BATCH = 4
SEQ = 2048
DEPTH = 4
DEC_BATCH = 128
DEC_SEQ = 4
PAST_LEN = 16384

D_A = D_MODEL // 2
CONV_A = 3
D_B = D_MODEL // 2
POOL_WINDOWS = (2, 4, 8, 16)
N_POOL = len(POOL_WINDOWS)
POOL_IN = D_B // N_POOL
POOL_OUT = D_MODEL // N_POOL
POOL_BUF = max(POOL_WINDOWS) - 1
D_C = D_MODEL // 2
CONV_C = 31
D_FF = 5632
EPS = 1e-6
W_IN_CG_START = 3 * D_A + D_B

F32 = jnp.float32
BF16 = jnp.bfloat16

V7X_SUBLANES = 8
V7X_LANES = 128
V7X_VREG_ELEMS = V7X_SUBLANES * V7X_LANES
V7X_VMEM_BYTES = 64 * 1024 * 1024
V7X_VMEM_USABLE_BYTES = V7X_VMEM_BYTES - 6 * 1024 * 1024

ROW_TILE = 512
COL_TILE = 512
C_ALL_ROWS = 144
MOD_COL_TILE = 1024
ACC_VREGS = 32
EPILOGUE_UNROLL = 4
FFN_SUB_TILE = 256


def _round_up(n, m):
    return -(-n // m) * m


def _chunk_rows(width):
    return ACC_VREGS * V7X_VREG_ELEMS // width


def _params(vmem_bytes, n_axes):
    limit = min(V7X_VMEM_USABLE_BYTES, int(vmem_bytes))
    return pltpu.CompilerParams(dimension_semantics=("arbitrary",) * n_axes, vmem_limit_bytes=limit)


def _dot(a, b):
    return jnp.dot(a, b, preferred_element_type=F32)


def _rows(m, r0, rc):
    if m.shape[0] == 1:
        return m[...]
    n_seq = m.shape[0]
    if rc <= n_seq:
        return m[pl.ds(r0 % n_seq if isinstance(r0, int) else lax.rem(r0, n_seq), rc), :]
    raise NotImplementedError


def _adaln(x, gain, scale, shift):
    y = x * lax.rsqrt(jnp.mean(x * x, axis=-1, keepdims=True) + EPS)
    return (y * gain) * (1.0 + scale) + shift


def _for_chunks(tm, rc, body):
    def step(c, carry):
        body(pl.multiple_of(c * rc, rc))
        return carry
    lax.fori_loop(0, tm // rc, step, 0)


class Group:
    def __init__(self, name, n_seq, t_len, time_major, p0):
        self.name = name
        self.n_seq = n_seq
        self.t_len = t_len
        self.time_major = time_major
        self.p0 = p0
        self.rows = n_seq * t_len
        self.tm = ROW_TILE
        self.n_tiles = self.rows // self.tm
        self.stride = n_seq if time_major else 1
        self.tps = 1 if time_major else t_len // self.tm
        assert self.rows % self.tm == 0
        assert (time_major and self.n_tiles == 1) or (not time_major and t_len % self.tm == 0)

    def hist_rows(self, k_hist):
        return _round_up(k_hist * self.stride, V7X_SUBLANES)

    def tail_rows(self, k_hist):
        return self.tm if self.time_major else _round_up(k_hist, V7X_SUBLANES)

    def mod_spec(self, which, row_axis):
        if self.time_major:
            return pl.BlockSpec((None, self.n_seq, D_MODEL), lambda *g: (g[-1][0], 0, which))
        tps = self.tps
        return pl.BlockSpec((None, None, 1, D_MODEL),
                            lambda *g: (g[-1][0], g[row_axis] // tps, 0, which))

    def next_mod_spec(self, which, row_axis):
        nxt = lambda l_ref: jnp.minimum(l_ref[0] + 1, DEPTH - 1)
        if self.time_major:
            return pl.BlockSpec((None, self.n_seq, D_MODEL), lambda *g: (nxt(g[-1]), 0, which))
        tps = self.tps
        return pl.BlockSpec((None, None, 1, D_MODEL),
                            lambda *g: (nxt(g[-1]), g[row_axis] // tps, 0, which))


PROMPT = Group("prompt", BATCH, SEQ, False, 0)
SAMPLE = Group("sample", DEC_BATCH, DEC_SEQ, True, PAST_LEN)
SH1, SC1, G1, SH2, SC2, G2 = range(6)


def _call(kernel, grp_name, name, grid, in_specs, out_specs, out_shape, scratch, vmem, l_arr, args):
    grid_spec = pltpu.PrefetchScalarGridSpec(
        num_scalar_prefetch=1, grid=grid, in_specs=in_specs, out_specs=out_specs,
        scratch_shapes=scratch)
    return pl.pallas_call(kernel, grid_spec=grid_spec, out_shape=out_shape,
                          compiler_params=_params(vmem, len(grid)),
                          name=f"{name}_{grp_name}")(l_arr, *args)


def _mod_kernel(c_ref, w_ref, b_ref, o_ref):
    c = c_ref[...]
    a = (c * jax.nn.sigmoid(c)).astype(BF16)
    o_ref[...] = _dot(a, w_ref[...].astype(BF16)) + b_ref[...]


def _modulation(c_all, ada_w, ada_b):
    n_cols = 6 * D_MODEL
    tn = MOD_COL_TILE
    vmem = 2 * (D_MODEL * tn * 4) + D_MODEL * tn * 2 + 4 * C_ALL_ROWS * (D_MODEL + 2 * tn) * 4 + (4 << 20)
    return pl.pallas_call(
        _mod_kernel,
        grid=(DEPTH, n_cols // tn),
        in_specs=[
            pl.BlockSpec((C_ALL_ROWS, D_MODEL), lambda l, n: (0, 0)),
            pl.BlockSpec((None, D_MODEL, tn), lambda l, n: (l, 0, n)),
            pl.BlockSpec((None, 1, tn), lambda l, n: (l, 0, n)),
        ],
        out_specs=pl.BlockSpec((None, C_ALL_ROWS, tn), lambda l, n: (l, 0, n)),
        out_shape=jax.ShapeDtypeStruct((DEPTH, C_ALL_ROWS, n_cols), F32),
        compiler_params=_params(vmem, 2),
        name="modulation",
    )(c_all, ada_w, ada_b.reshape(DEPTH, 1, n_cols))


def _norm_kernel(l_ref, x_ref, gain_ref, sc_ref, sh_ref, o_ref, *, tm):
    rc = _chunk_rows(D_MODEL)

    def body(r0):
        o_ref[pl.ds(r0, rc), :] = _adaln(x_ref[pl.ds(r0, rc), :], gain_ref[...], _rows(sc_ref, r0, rc),
                                         _rows(sh_ref, r0, rc)).astype(BF16)
    _for_chunks(tm, rc, body)


def _norm(grp, l_arr, x, gain, mod):
    tm = grp.tm
    row = pl.BlockSpec((tm, D_MODEL), lambda i, l: (i, 0))
    in_specs = [row, pl.BlockSpec((1, D_MODEL), lambda i, l: (0, 0)),
                grp.mod_spec(SC1, 0), grp.mod_spec(SH1, 0)]
    return _call(functools.partial(_norm_kernel, tm=tm), grp.name, "norm_mod", (grp.n_tiles,),
                 in_specs, row, jax.ShapeDtypeStruct((grp.rows, D_MODEL), BF16), [],
                 6 * tm * D_MODEL * 4 + (8 << 20), l_arr, [x, gain, mod, mod])


def _hist_begin(xx_ref, hp, tm, t_in_seq):
    @pl.when(t_in_seq == 0)
    def _():
        xx_ref[pl.ds(0, hp), :] = jnp.zeros((hp, xx_ref.shape[1]), F32)

    @pl.when(t_in_seq != 0)
    def _():
        xx_ref[pl.ds(0, hp), :] = xx_ref[pl.ds(tm, hp), :]


def _tap_rowmajor(xx_ref, hp, k_hist, r0, rc, cols=slice(None)):
    return lambda j: xx_ref[pl.ds(r0 + hp - (k_hist - j), rc), cols]


def _tap_timemajor(st_ref, new_ref, n_seq, k_hist, t, cols=slice(None)):
    def tap(j):
        step = t + j
        if step < k_hist:
            return st_ref[pl.ds(step * n_seq, n_seq), cols]
        return new_ref[pl.ds((step - k_hist) * n_seq, n_seq), cols]
    return tap


def _dwconv(tap, w_ref, k_w):
    acc = None
    for k in range(k_w):
        term = tap(k) * w_ref[k:k + 1, :]
        acc = term if acc is None else acc + term
    return acc


def _tail_store(grp, tail_ref, xx_ref, hp, k_hist, t_in_seq):
    r = grp.tail_rows(k_hist)

    @pl.when(t_in_seq == grp.tps - 1)
    def _():
        tail_ref[...] = xx_ref[pl.ds(hp + grp.tm - r, r), :]


def _tile_chunks(grp, width):
    if grp.time_major:
        assert grp.n_seq * width <= 4 * ACC_VREGS * V7X_VREG_ELEMS
        return [(t * grp.n_seq, grp.n_seq, t) for t in range(grp.t_len)]
    rc = _chunk_rows(width)
    return [(r0, rc, None) for r0 in range(0, grp.tm, rc)]


def _branch_a_kernel(l_ref, *refs, grp, tc):
    if grp.time_major:
        h_ref, wb_ref, wc_ref, wv_ref, cw_ref, st_ref, ya_ref, new_ref, bg_ref, w16_ref = refs
    else:
        h_ref, wb_ref, wc_ref, wv_ref, cw_ref, ya_ref, tail_ref, xx_ref, bg_ref, w16_ref = refs
    tm, k_hist = grp.tm, CONV_A - 1

    @pl.when(pl.program_id(1) == 0)
    def _():
        for n, w_ref in enumerate((wb_ref, wc_ref, wv_ref)):
            w16_ref[n] = w_ref[...].astype(BF16)

    h = h_ref[...]
    bg_ref[...] = _dot(h, w16_ref[0])
    cv = _dot(h, w16_ref[1]) * _dot(h, w16_ref[2])
    if grp.time_major:
        new_ref[...] = cv
    else:
        t_in_seq = pl.program_id(1) % grp.tps
        hp = grp.hist_rows(k_hist)
        _hist_begin(xx_ref, hp, tm, t_in_seq)
        xx_ref[pl.ds(hp, tm), :] = cv
        _tail_store(grp, tail_ref, xx_ref, hp, k_hist, t_in_seq)
    for r0, rc, t in _tile_chunks(grp, tc):
        if grp.time_major:
            tap = _tap_timemajor(st_ref, new_ref, grp.n_seq, k_hist, t)
        else:
            tap = _tap_rowmajor(xx_ref, hp, k_hist, r0, rc)
        ya_ref[pl.ds(r0, rc), :] = (bg_ref[pl.ds(r0, rc), :] * _dwconv(tap, cw_ref, CONV_A)).astype(BF16)


def _branch_a(grp, l_arr, h, w_in, conv_w, state):
    tm, tc, k_hist = grp.tm, COL_TILE, CONV_A - 1
    nq = D_A // tc
    off_c, off_v = D_A // tc, 2 * D_A // tc
    in_specs = [
        pl.BlockSpec((tm, D_MODEL), lambda q, i, l: (i, 0)),
        pl.BlockSpec((None, D_MODEL, tc), lambda q, i, l: (l[0], 0, q)),
        pl.BlockSpec((None, D_MODEL, tc), lambda q, i, l: (l[0], 0, off_c + q)),
        pl.BlockSpec((None, D_MODEL, tc), lambda q, i, l: (l[0], 0, off_v + q)),
        pl.BlockSpec((None, CONV_A, tc), lambda q, i, l: (l[0], 0, q)),
    ]
    args = [h, w_in, w_in, w_in, conv_w]
    scratch = []
    if grp.time_major:
        in_specs.append(pl.BlockSpec((None, k_hist * grp.n_seq, tc), lambda q, i, l: (l[0], 0, q)))
        args.append(state)
        tail_spec = pl.BlockSpec((tm, tc), lambda q, i, l: (0, q))
        tail_shape = jax.ShapeDtypeStruct((tm, D_A), F32)
    else:
        tps = grp.tps
        r = grp.tail_rows(k_hist)
        scratch.append(pltpu.VMEM((grp.hist_rows(k_hist) + tm, tc), F32))
        tail_spec = pl.BlockSpec((None, r, tc), lambda q, i, l: (i // tps, 0, q))
        tail_shape = jax.ShapeDtypeStruct((grp.n_seq, r, D_A), F32)
    scratch.append(pltpu.VMEM((tm, tc), F32))
    scratch.append(pltpu.VMEM((3, D_MODEL, tc), BF16))
    vmem = (2 * (tm * D_MODEL * 2 + 3 * D_MODEL * tc * 4 + tm * tc * 2 + tm * tc * 4) + 3 * D_MODEL * tc * 2
            + 10 * tm * tc * 4 + (6 << 20))
    return _call(functools.partial(_branch_a_kernel, grp=grp, tc=tc), grp.name, "branch_a",
                 (nq, grp.n_tiles), in_specs,
                 [pl.BlockSpec((tm, tc), lambda q, i, l: (i, q)), tail_spec],
                 [jax.ShapeDtypeStruct((grp.rows, D_A), BF16), tail_shape], scratch, vmem, l_arr, args)


def _branch_b_kernel(l_ref, *refs, grp):
    if grp.time_major:
        h_ref, w_ref, st_ref, diff_ref, new_ref, w16_ref = refs
    else:
        h_ref, w_ref, diff_ref, tail_ref, xx_ref, w16_ref = refs
    tm, k_hist = grp.tm, POOL_BUF

    @pl.when(pl.program_id(0) == 0)
    def _():
        w16_ref[...] = w_ref[...].astype(BF16)

    u = _dot(h_ref[...], w16_ref[...])
    if grp.time_major:
        new_ref[...] = u
    else:
        t_in_seq = pl.program_id(0) % grp.tps
        hp = grp.hist_rows(k_hist)
        _hist_begin(xx_ref, hp, tm, t_in_seq)
        xx_ref[pl.ds(hp, tm), :] = u
        _tail_store(grp, tail_ref, xx_ref, hp, k_hist, t_in_seq)
    for g, win in enumerate(POOL_WINDOWS):
        cols = slice(g * POOL_IN, (g + 1) * POOL_IN)
        for r0, rc, t in _tile_chunks(grp, POOL_IN):
            if grp.time_major:
                tap = _tap_timemajor(st_ref, new_ref, grp.n_seq, k_hist, t, cols)
                assert grp.p0 + 1 >= win
                cnt = float(win)
            else:
                tap = _tap_rowmajor(xx_ref, hp, k_hist, r0, rc, cols)
                pos = grp.p0 + t_in_seq * tm + r0 + lax.broadcasted_iota(jnp.int32, (rc, 1), 0)
                cnt = jnp.minimum(pos + 1, win).astype(F32)
            cur = tap(k_hist)
            total = cur
            for j in range(1, win):
                total = total + tap(k_hist - j)
            diff_ref[pl.ds(r0, rc), cols] = (total / cnt - cur).astype(BF16)


def _branch_b(grp, l_arr, h, w_in, state):
    tm, k_hist = grp.tm, POOL_BUF
    off = 3 * D_A // D_B
    in_specs = [
        pl.BlockSpec((tm, D_MODEL), lambda i, l: (i, 0)),
        pl.BlockSpec((None, D_MODEL, D_B), lambda i, l: (l[0], 0, off)),
    ]
    args = [h, w_in]
    scratch = []
    if grp.time_major:
        in_specs.append(pl.BlockSpec((None, k_hist * grp.n_seq, D_B), lambda i, l: (l[0], 0, 0)))
        args.append(state)
        tail_spec = pl.BlockSpec((tm, D_B), lambda i, l: (0, 0))
        tail_shape = jax.ShapeDtypeStruct((tm, D_B), F32)
        state_bytes = 2 * k_hist * grp.n_seq * D_B * 4
    else:
        tps = grp.tps
        r = grp.tail_rows(k_hist)
        scratch.append(pltpu.VMEM((grp.hist_rows(k_hist) + tm, D_B), F32))
        tail_spec = pl.BlockSpec((None, r, D_B), lambda i, l: (i // tps, 0, 0))
        tail_shape = jax.ShapeDtypeStruct((grp.n_seq, r, D_B), F32)
        state_bytes = 0
    scratch.append(pltpu.VMEM((D_MODEL, D_B), BF16))
    vmem = (2 * (tm * D_MODEL * 2 + D_MODEL * D_B * 4 + tm * D_B * 2 + tm * D_B * 4) + D_MODEL * D_B * 2
            + 8 * tm * D_B * 4 + state_bytes + (6 << 20))
    return _call(functools.partial(_branch_b_kernel, grp=grp), grp.name, "branch_b",
                 (grp.n_tiles,), in_specs,
                 [pl.BlockSpec((tm, D_B), lambda i, l: (i, 0)), tail_spec],
                 [jax.ShapeDtypeStruct((grp.rows, D_B), BF16), tail_shape], scratch, vmem, l_arr, args)


def _branch_c_kernel(l_ref, *refs, grp):
    if grp.time_major:
        h_ref, w1_ref, w2_ref, cw_ref, cb_ref, lg_ref, lb_ref, st_ref, z_ref, new_ref = refs
    else:
        h_ref, w1_ref, w2_ref, cw_ref, cb_ref, lg_ref, lb_ref, z_ref, tail_ref, xs_ref, v_ref, cwb_ref = refs
    tm, k_hist = grp.tm, CONV_C - 1

    def norm_swish(v):
        mu = jnp.mean(v, axis=-1, keepdims=True)
        vc = v - mu
        var = jnp.mean(vc * vc, axis=-1, keepdims=True)
        y = vc * lax.rsqrt(var + EPS) * lg_ref[...] + lb_ref[...]
        return (y * jax.nn.sigmoid(y)).astype(BF16)

    h = h_ref[...]
    glu = _dot(h, w1_ref[...]) * jax.nn.sigmoid(_dot(h, w2_ref[...]))
    if grp.time_major:
        new_ref[...] = glu
    else:
        t_in_seq = pl.program_id(0) % grp.tps
        hp = grp.hist_rows(k_hist)
        x0 = xs_ref.at[0]
        _hist_begin(x0, hp, tm, t_in_seq)
        x0[pl.ds(hp, tm), :] = glu
        x0[pl.ds(hp + tm, V7X_SUBLANES), :] = jnp.zeros((V7X_SUBLANES, D_C), F32)
        _tail_store(grp, tail_ref, x0, hp, k_hist, t_in_seq)
        rc_copy = _chunk_rows(D_C)
        for b in range(1, V7X_SUBLANES):
            for r0 in range(0, hp + tm, rc_copy):
                xs_ref[b, pl.ds(r0, rc_copy), :] = x0[pl.ds(r0 + b, rc_copy), :]

        rc = _chunk_rows(D_C)

        @pl.when(pl.program_id(0) == 0)
        def _():
            for k in range(CONV_C):
                cwb_ref[k] = jnp.broadcast_to(cw_ref[k:k + 1, :], (V7X_SUBLANES, D_C))

        def conv_chunk(r0):
            groups = range(0, rc, V7X_SUBLANES)
            accs = {}
            for k in range(CONV_C):
                a, b = divmod(hp - k_hist + k, V7X_SUBLANES)
                w_k = cwb_ref[k]
                for g in groups:
                    term = xs_ref[b, pl.ds(r0 + V7X_SUBLANES * a + g, V7X_SUBLANES), :] * w_k
                    accs[g] = term if k == 0 else accs[g] + term
            for g in groups:
                v_ref[pl.ds(r0 + g, V7X_SUBLANES), :] = accs[g] + cb_ref[...]
        _for_chunks(tm, rc, conv_chunk)

        def norm_chunks(r_base):
            for u in range(EPILOGUE_UNROLL):
                r0 = r_base + u * rc
                z_ref[pl.ds(r0, rc), :] = norm_swish(v_ref[pl.ds(r0, rc), :])
        _for_chunks(tm, EPILOGUE_UNROLL * rc, norm_chunks)
        return
    for r0, rc, t in _tile_chunks(grp, D_C):
        tap = _tap_timemajor(st_ref, new_ref, grp.n_seq, k_hist, t)
        z_ref[pl.ds(r0, rc), :] = norm_swish(_dwconv(tap, cw_ref, CONV_C) + cb_ref[...])


def _branch_c(grp, l_arr, h, w_in, conv_w, conv_b, ln_g, ln_b, state):
    tm, k_hist = grp.tm, CONV_C - 1
    off1 = (3 * D_A + D_B - W_IN_CG_START) // D_C
    vec = pl.BlockSpec((None, 1, D_C), lambda i, l: (l[0], 0, 0))
    in_specs = [
        pl.BlockSpec((tm, D_MODEL), lambda i, l: (i, 0)),
        pl.BlockSpec((None, D_MODEL, D_C), lambda i, l: (l[0], 0, off1)),
        pl.BlockSpec((None, D_MODEL, D_C), lambda i, l: (l[0], 0, off1 + 1)),
        pl.BlockSpec((None, CONV_C, D_C), lambda i, l: (l[0], 0, 0)),
        vec, vec, vec,
    ]
    args = [h, w_in, w_in, conv_w, conv_b, ln_g, ln_b]
    scratch = []
    if grp.time_major:
        in_specs.append(pl.BlockSpec((None, k_hist * grp.n_seq, D_C), lambda i, l: (l[0], 0, 0),
                                     pipeline_mode=pl.Buffered(1)))
        args.append(state)
        tail_spec = pl.BlockSpec((tm, D_C), lambda i, l: (0, 0))
        tail_shape = jax.ShapeDtypeStruct((tm, D_C), F32)
        state_bytes = k_hist * grp.n_seq * D_C * 4
    else:
        tps = grp.tps
        r = grp.tail_rows(k_hist)
        shifted_rows = grp.hist_rows(k_hist) + tm + V7X_SUBLANES
        scratch.append(pltpu.VMEM((V7X_SUBLANES, shifted_rows, D_C), F32))
        scratch.append(pltpu.VMEM((tm, D_C), F32))
        scratch.append(pltpu.VMEM((CONV_C, V7X_SUBLANES, D_C), F32))
        tail_spec = pl.BlockSpec((None, r, D_C), lambda i, l: (i // tps, 0, 0))
        tail_shape = jax.ShapeDtypeStruct((grp.n_seq, r, D_C), F32)
        state_bytes = V7X_SUBLANES * shifted_rows * D_C * 4
    vmem = (2 * (tm * D_MODEL * 2 + 2 * D_MODEL * D_C * 2 + tm * D_C * 2 + tm * D_C * 4)
            + 10 * tm * D_C * 4 + state_bytes + (6 << 20))
    return _call(functools.partial(_branch_c_kernel, grp=grp), grp.name, "branch_c",
                 (grp.n_tiles,), in_specs,
                 [pl.BlockSpec((tm, D_C), lambda i, l: (i, 0)), tail_spec],
                 [jax.ShapeDtypeStruct((grp.rows, D_C), BF16), tail_shape], scratch, vmem, l_arr, args)


def _residual_epilogue(x_ref, gate_ref, acc_ref, ng_ref, sc_ref, sh_ref, xo_ref, ho_ref, tm):
    rc = _chunk_rows(D_MODEL)

    def body(r_base):
        for u in range(EPILOGUE_UNROLL):
            r0 = r_base + u * rc
            x = x_ref[pl.ds(r0, rc), :] + _rows(gate_ref, r0, rc) * acc_ref[pl.ds(r0, rc), :]
            xo_ref[pl.ds(r0, rc), :] = x
            ho_ref[pl.ds(r0, rc), :] = _adaln(x, ng_ref[...], _rows(sc_ref, r0, rc),
                                              _rows(sh_ref, r0, rc)).astype(BF16)
    _for_chunks(tm, EPILOGUE_UNROLL * rc, body)


def _merge_kernel(l_ref, h_ref, ya_ref, df_ref, z_ref, woa_ref, pw_ref, ps_ref, wp2_ref, bp2_ref,
                  wg0_ref, wg1_ref, wg2_ref, m_ref):
    h = h_ref[...]
    y_a = _dot(ya_ref[...], woa_ref[...])
    y_b = _dot(df_ref[...], pw_ref[...]) * ps_ref[...]
    y_c = _dot(z_ref[...], wp2_ref[...]) + bp2_ref[...]
    merged = (jax.nn.sigmoid(_dot(h, wg0_ref[...])) * y_a
              + jax.nn.sigmoid(_dot(h, wg1_ref[...])) * y_b
              + jax.nn.sigmoid(_dot(h, wg2_ref[...])) * y_c)
    m_ref[...] = merged.astype(BF16)


def _merge(grp, l_arr, h, ya, diffs, z, w_in, w_out_a, pool_w, pool_scale, w_pw2, b_pw2):
    tm, tn = grp.tm, POOL_OUT
    nj = D_MODEL // tn
    off_g = (3 * D_A + D_B + 2 * D_C - W_IN_CG_START) // tn
    per_g = D_MODEL // tn
    row = lambda c: pl.BlockSpec((tm, c), lambda j, i, l: (i, 0))
    in_specs = [
        row(D_MODEL), row(D_A),
        pl.BlockSpec((tm, POOL_IN), lambda j, i, l: (i, j)),
        row(D_C),
        pl.BlockSpec((None, D_A, tn), lambda j, i, l: (l[0], 0, j)),
        pl.BlockSpec((None, None, POOL_IN, tn), lambda j, i, l: (l[0], j, 0, 0)),
        pl.BlockSpec((None, 1, tn), lambda j, i, l: (l[0], 0, j)),
        pl.BlockSpec((None, D_C, tn), lambda j, i, l: (l[0], 0, j)),
        pl.BlockSpec((None, 1, tn), lambda j, i, l: (l[0], 0, j)),
        pl.BlockSpec((None, D_MODEL, tn), lambda j, i, l: (l[0], 0, off_g + j)),
        pl.BlockSpec((None, D_MODEL, tn), lambda j, i, l: (l[0], 0, off_g + per_g + j)),
        pl.BlockSpec((None, D_MODEL, tn), lambda j, i, l: (l[0], 0, off_g + 2 * per_g + j)),
    ]
    act_bytes = tm * (D_MODEL + D_A + POOL_IN + D_C + tn) * 2
    w_bytes = (D_A + POOL_IN + D_C + 3 * D_MODEL) * tn * 2
    vmem = 2 * (act_bytes + w_bytes) + 16 * tm * tn * 4 + (4 << 20)
    args = [h, ya, diffs, z, w_out_a, pool_w, pool_scale, w_pw2, b_pw2, w_in, w_in, w_in]
    return _call(_merge_kernel, grp.name, "merge", (nj, grp.n_tiles), in_specs,
                 pl.BlockSpec((tm, tn), lambda j, i, l: (i, j)),
                 jax.ShapeDtypeStruct((grp.rows, D_MODEL), BF16), [], vmem, l_arr, args)


def _out_proj_kernel(l_ref, m_ref, x_ref, g1_ref, sc2_ref, sh2_ref, ng_ref, wo_ref,
                     xo_ref, ho_ref, acc_ref, *, tm):
    acc_ref[...] = _dot(m_ref[...], wo_ref[...])
    _residual_epilogue(x_ref, g1_ref, acc_ref, ng_ref, sc2_ref, sh2_ref, xo_ref, ho_ref, tm)


def _out_proj(grp, l_arr, merged, x, mod, norm_g, w_o):
    tm = grp.tm
    row = pl.BlockSpec((tm, D_MODEL), lambda i, l: (i, 0))
    in_specs = [
        row, row, grp.mod_spec(G1, 0), grp.mod_spec(SC2, 0), grp.mod_spec(SH2, 0),
        pl.BlockSpec((None, 1, D_MODEL), lambda i, l: (l[0], 0, 0)),
        pl.BlockSpec((None, D_MODEL, D_MODEL), lambda i, l: (l[0], 0, 0), pipeline_mode=pl.Buffered(1)),
    ]
    vmem = (2 * tm * D_MODEL * (2 + 4 + 4 + 2) + D_MODEL * D_MODEL * 2 + 3 * tm * D_MODEL * 4
            + (4 << 20))
    return _call(functools.partial(_out_proj_kernel, tm=tm), grp.name, "out_proj",
                 (grp.n_tiles,), in_specs, [row, row],
                 [jax.ShapeDtypeStruct((grp.rows, D_MODEL), F32),
                  jax.ShapeDtypeStruct((grp.rows, D_MODEL), BF16)],
                 [pltpu.VMEM((tm, D_MODEL), F32)], vmem, l_arr, [merged, x, mod, mod, mod, norm_g, w_o])


def _final_epilogue(x_ref, gate_ref, acc_ref, ng_ref, y_ref, tm):
    rc = _chunk_rows(D_MODEL)

    def body(r_base):
        for u in range(EPILOGUE_UNROLL):
            r0 = r_base + u * rc
            x = x_ref[pl.ds(r0, rc), :] + _rows(gate_ref, r0, rc) * acc_ref[pl.ds(r0, rc), :]
            y_ref[pl.ds(r0, rc), :] = (
                x * lax.rsqrt(jnp.mean(x * x, axis=-1, keepdims=True) + EPS) * ng_ref[...])
    _for_chunks(tm, EPILOGUE_UNROLL * rc, body)


def _ffn_kernel(l_ref, h_ref, x_ref, g2_ref, wg_ref, wu_ref, wd_ref, ng_ref, *rest, tm, nk, last):
    if last:
        y_ref, acc_ref = rest
    else:
        sc_ref, sh_ref, xo_ref, ho_ref, acc_ref = rest
    k = pl.program_id(1)

    @pl.when(k == 0)
    def _():
        acc_ref[...] = jnp.zeros_like(acc_ref)

    h = h_ref[...]
    tn = wg_ref.shape[1]
    part = None
    for c0 in range(0, tn, FFN_SUB_TILE):
        cols = slice(c0, c0 + FFN_SUB_TILE)
        gate = _dot(h, wg_ref[:, cols])
        act = (gate * jax.nn.sigmoid(gate)) * _dot(h, wu_ref[:, cols])
        p = _dot(act.astype(BF16), wd_ref[cols, :])
        part = p if part is None else part + p
    acc_ref[...] += part

    @pl.when(k == nk - 1)
    def _():
        if last:
            _final_epilogue(x_ref, g2_ref, acc_ref, ng_ref, y_ref, tm)
        else:
            _residual_epilogue(x_ref, g2_ref, acc_ref, ng_ref, sc_ref, sh_ref, xo_ref, ho_ref, tm)


def _ffn(grp, l_arr, h, x, mod, w_gate_up, w_down, gain, last):
    tm, tn = grp.tm, COL_TILE
    nk = D_FF // tn
    row = pl.BlockSpec((tm, D_MODEL), lambda i, k, l: (i, 0))
    in_specs = [
        row, row, grp.mod_spec(G2, 0),
        pl.BlockSpec((None, D_MODEL, tn), lambda i, k, l: (l[0], 0, k)),
        pl.BlockSpec((None, D_MODEL, tn), lambda i, k, l: (l[0], 0, nk + k)),
        pl.BlockSpec((None, tn, D_MODEL), lambda i, k, l: (l[0], k, 0)),
    ]
    args = [h, x, mod, w_gate_up, w_gate_up, w_down, gain]
    if last:
        in_specs.append(pl.BlockSpec((1, D_MODEL), lambda i, k, l: (0, 0)))
        out_specs = [row]
        out_shape = [jax.ShapeDtypeStruct((grp.rows, D_MODEL), F32)]
    else:
        in_specs += [pl.BlockSpec((None, 1, D_MODEL), lambda i, k, l: (l[0] + 1, 0, 0)),
                     grp.next_mod_spec(SC1, 0), grp.next_mod_spec(SH1, 0)]
        args += [mod, mod]
        out_specs = [row, row]
        out_shape = [jax.ShapeDtypeStruct((grp.rows, D_MODEL), F32),
                     jax.ShapeDtypeStruct((grp.rows, D_MODEL), BF16)]
    vmem = (2 * (tm * D_MODEL * (2 + 4) + 3 * D_MODEL * tn * 2 + tm * D_MODEL * (4 + 2))
            + tm * D_MODEL * 4 + 10 * tm * tn * 4 + (4 << 20))
    return _call(functools.partial(_ffn_kernel, tm=tm, nk=nk, last=last), grp.name,
                 "ffn_last" if last else "ffn", (grp.n_tiles, nk), in_specs, out_specs, out_shape,
                 [pltpu.VMEM((tm, D_MODEL), F32)], vmem, l_arr, args)


def _layer(grp, l_arr, x, h, mod, states, p, last):
    st_a, st_p, st_c = states if states is not None else (None, None, None)
    ya, tail_a = _branch_a(grp, l_arr, h, p["w_in_f32"], p["conv_a_w"], st_a)
    diffs, tail_p = _branch_b(grp, l_arr, h, p["w_in_f32"], st_p)
    z, tail_c = _branch_c(grp, l_arr, h, p["w_in_cg"], p["conv_c_w"], p["conv_c_b"], p["ln_c_g"],
                          p["ln_c_b"], st_c)
    merged = _merge(grp, l_arr, h, ya, diffs, z, p["w_in_cg"], p["w_out_a"], p["pool_w"],
                    p["pool_scale"], p["w_pw2"], p["b_pw2"])
    x, h = _out_proj(grp, l_arr, merged, x, mod, p["norm_ffn_g"], p["w_o"])
    tails = (tail_a, tail_p, tail_c)
    if last:
        (y,) = _ffn(grp, l_arr, h, x, mod, p["w_gate_up"], p["w_down"], p["final_norm_g"], True)
        return y, tails
    x, h = _ffn(grp, l_arr, h, x, mod, p["w_gate_up"], p["w_down"], p["norm_mix_g"], False)
    return x, h, tails


def kernel(x_prompt, x_sample, c_prompt, c_sample, state_conv_a, state_pool, state_conv_c,
           ada_w, ada_b, norm_mix_g, w_in, conv_a_w, w_out_a, pool_w, pool_scale,
           conv_c_w, conv_c_b, ln_c_g, ln_c_b, w_pw2, b_pw2, w_o, norm_ffn_g,
           w_gate_up, w_down, final_norm_g):
    vec3 = lambda a: a.reshape(DEPTH, 1, a.shape[-1])
    p = {
        "w_in_f32": w_in, "w_in_cg": w_in[:, :, W_IN_CG_START:].astype(BF16),
        "w_out_a": w_out_a.astype(BF16), "pool_w": pool_w.astype(BF16),
        "w_pw2": w_pw2.astype(BF16), "w_o": w_o.astype(BF16), "w_gate_up": w_gate_up.astype(BF16),
        "w_down": w_down.astype(BF16),
        "conv_a_w": conv_a_w, "conv_c_w": conv_c_w,
        "conv_c_b": vec3(conv_c_b), "ln_c_g": vec3(ln_c_g), "ln_c_b": vec3(ln_c_b),
        "pool_scale": vec3(pool_scale), "b_pw2": vec3(b_pw2), "norm_ffn_g": vec3(norm_ffn_g),
        "norm_mix_g": vec3(norm_mix_g), "final_norm_g": final_norm_g.reshape(1, D_MODEL),
    }

    c_all = jnp.concatenate(
        [c_sample, c_prompt, jnp.zeros((C_ALL_ROWS - DEC_BATCH - BATCH, D_MODEL), F32)], axis=0)
    mod_s = _modulation(c_all, ada_w, ada_b)
    mod_p = mod_s[:, DEC_BATCH:DEC_BATCH + BATCH].reshape(DEPTH, BATCH, 1, 6 * D_MODEL)

    def time_major(s):
        s = jnp.swapaxes(s, -3, -2)
        return s.reshape(s.shape[:-3] + (-1, s.shape[-1]))

    states_s = (time_major(state_conv_a), time_major(state_pool), time_major(state_conv_c))
    x_p = x_prompt.reshape(BATCH * SEQ, D_MODEL)
    x_s = time_major(x_sample)

    l0 = jnp.zeros((1,), jnp.int32)
    h_p = _norm(PROMPT, l0, x_p, norm_mix_g[0:1], mod_p)
    h_s = _norm(SAMPLE, l0, x_s, norm_mix_g[0:1], mod_s)

    def body(carry, l):
        x_p, h_p, x_s, h_s = carry
        l_arr = jnp.reshape(l, (1,)).astype(jnp.int32)
        x_p, h_p, tails_p = _layer(PROMPT, l_arr, x_p, h_p, mod_p, None, p, False)
        x_s, h_s, tails_s = _layer(SAMPLE, l_arr, x_s, h_s, mod_s, states_s, p, False)
        return (x_p, h_p, x_s, h_s), (tails_p, tails_s)

    (x_p, h_p, x_s, h_s), tails = lax.scan(body, (x_p, h_p, x_s, h_s), jnp.arange(DEPTH - 1))
    l_last = jnp.full((1,), DEPTH - 1, jnp.int32)
    y_p, last_p = _layer(PROMPT, l_last, x_p, h_p, mod_p, None, p, True)
    y_s, last_s = _layer(SAMPLE, l_last, x_s, h_s, mod_s, states_s, p, True)
    tails_p, tails_s = jax.tree.map(lambda a, b: jnp.concatenate([a, b[None]], axis=0),
                                    tails, (last_p, last_s))

    def from_time_major(a):
        a = a.reshape(a.shape[:-2] + (DEC_SEQ, DEC_BATCH, a.shape[-1]))
        return jnp.swapaxes(a, -3, -2)

    hist = (CONV_A - 1, POOL_BUF, CONV_C - 1)
    old_s = (state_conv_a, state_pool, state_conv_c)
    new_p = [tails_p[b][:, :, -hist[b]:, :] for b in range(3)]
    new_s = [jnp.concatenate([old_s[b], from_time_major(tails_s[b])], axis=2)[:, :, -hist[b]:, :]
             for b in range(3)]
    return (y_p.reshape(BATCH, SEQ, D_MODEL), from_time_major(y_s),
            new_p[0], new_p[1], new_p[2], new_s[0], new_s[1], new_s[2])
```

```python
import functools

import jax
import jax.numpy as jnp
from jax import lax
from jax.experimental import pallas as pl
from jax.experimental.pallas import tpu as pltpu

D_MODEL = 2048
BATCH = 4
SEQ = 2048
DEPTH = 4
DEC_BATCH = 128
DEC_SEQ = 4
PAST_LEN = 16384

D_A = D_MODEL // 2
CONV_A = 3
D_B = D_MODEL // 2
POOL_WINDOWS = (2, 4, 8, 16)
N_POOL = len(POOL_WINDOWS)
POOL_IN = D_B // N_POOL
POOL_OUT = D_MODEL // N_POOL
POOL_BUF = max(POOL_WINDOWS) - 1
D_C = D_MODEL // 2
CONV_C = 31
D_FF = 5632
EPS = 1e-6
W_IN_CG_START = 3 * D_A + D_B

F32 = jnp.float32
BF16 = jnp.bfloat16

V7X_SUBLANES = 8
V7X_LANES = 128
V7X_VREG_ELEMS = V7X_SUBLANES * V7X_LANES
V7X_VMEM_BYTES = 64 * 1024 * 1024
V7X_VMEM_USABLE_BYTES = V7X_VMEM_BYTES - 6 * 1024 * 1024

ROW_TILE = 512
COL_TILE = 512
C_ALL_ROWS = 144
MOD_COL_TILE = 1024
ACC_VREGS = 32
EPILOGUE_UNROLL = 4
FFN_SUB_TILE = 256
MERGE_SUB_TILE = 256


def _round_up(n, m):
    return -(-n // m) * m


def _chunk_rows(width):
    return ACC_VREGS * V7X_VREG_ELEMS // width


def _params(vmem_bytes, n_axes):
    limit = min(V7X_VMEM_USABLE_BYTES, int(vmem_bytes))
    return pltpu.CompilerParams(dimension_semantics=("arbitrary",) * n_axes, vmem_limit_bytes=limit)


def _dot(a, b):
    return jnp.dot(a, b, preferred_element_type=F32)


def _rows(m, r0, rc):
    if m.shape[0] == 1:
        return m[...]
    n_seq = m.shape[0]
    if rc <= n_seq:
        return m[pl.ds(r0 % n_seq if isinstance(r0, int) else lax.rem(r0, n_seq), rc), :]
    raise NotImplementedError


def _adaln(x, gain, scale, shift):
    y = x * lax.rsqrt(jnp.mean(x * x, axis=-1, keepdims=True) + EPS)
    return (y * gain) * (1.0 + scale) + shift


def _for_chunks(tm, rc, body):
    def step(c, carry):
        body(pl.multiple_of(c * rc, rc))
        return carry
    lax.fori_loop(0, tm // rc, step, 0)


class Group:
    def __init__(self, name, n_seq, t_len, time_major, p0):
        self.name = name
        self.n_seq = n_seq
        self.t_len = t_len
        self.time_major = time_major
        self.p0 = p0
        self.rows = n_seq * t_len
        self.tm = ROW_TILE
        self.n_tiles = self.rows // self.tm
        self.stride = n_seq if time_major else 1
        self.tps = 1 if time_major else t_len // self.tm
        assert self.rows % self.tm == 0
        assert (time_major and self.n_tiles == 1) or (not time_major and t_len % self.tm == 0)

    def hist_rows(self, k_hist):
        return _round_up(k_hist * self.stride, V7X_SUBLANES)

    def tail_rows(self, k_hist):
        return self.tm if self.time_major else _round_up(k_hist, V7X_SUBLANES)

    def mod_spec(self, which, row_axis):
        if self.time_major:
            return pl.BlockSpec((None, self.n_seq, D_MODEL), lambda *g: (g[-1][0], 0, which))
        tps = self.tps
        return pl.BlockSpec((None, None, 1, D_MODEL),
                            lambda *g: (g[-1][0], g[row_axis] // tps, 0, which))

    def next_mod_spec(self, which, row_axis):
        nxt = lambda l_ref: jnp.minimum(l_ref[0] + 1, DEPTH - 1)
        if self.time_major:
            return pl.BlockSpec((None, self.n_seq, D_MODEL), lambda *g: (nxt(g[-1]), 0, which))
        tps = self.tps
        return pl.BlockSpec((None, None, 1, D_MODEL),
                            lambda *g: (nxt(g[-1]), g[row_axis] // tps, 0, which))


PROMPT = Group("prompt", BATCH, SEQ, False, 0)
SAMPLE = Group("sample", DEC_BATCH, DEC_SEQ, True, PAST_LEN)
SH1, SC1, G1, SH2, SC2, G2 = range(6)


def _call(kernel, grp_name, name, grid, in_specs, out_specs, out_shape, scratch, vmem, l_arr, args):
    grid_spec = pltpu.PrefetchScalarGridSpec(
        num_scalar_prefetch=1, grid=grid, in_specs=in_specs, out_specs=out_specs,
        scratch_shapes=scratch)
    return pl.pallas_call(kernel, grid_spec=grid_spec, out_shape=out_shape,
                          compiler_params=_params(vmem, len(grid)),
                          name=f"{name}_{grp_name}")(l_arr, *args)


def _mod_kernel(c_ref, w_ref, b_ref, o_ref):
    c = c_ref[...]
    a = (c * jax.nn.sigmoid(c)).astype(BF16)
    o_ref[...] = _dot(a, w_ref[...].astype(BF16)) + b_ref[...]


def _modulation(c_all, ada_w, ada_b):
    n_cols = 6 * D_MODEL
    tn = MOD_COL_TILE
    vmem = 2 * (D_MODEL * tn * 4) + D_MODEL * tn * 2 + 4 * C_ALL_ROWS * (D_MODEL + 2 * tn) * 4 + (4 << 20)
    return pl.pallas_call(
        _mod_kernel,
        grid=(DEPTH, n_cols // tn),
        in_specs=[
            pl.BlockSpec((C_ALL_ROWS, D_MODEL), lambda l, n: (0, 0)),
            pl.BlockSpec((None, D_MODEL, tn), lambda l, n: (l, 0, n)),
            pl.BlockSpec((None, 1, tn), lambda l, n: (l, 0, n)),
        ],
        out_specs=pl.BlockSpec((None, C_ALL_ROWS, tn), lambda l, n: (l, 0, n)),
        out_shape=jax.ShapeDtypeStruct((DEPTH, C_ALL_ROWS, n_cols), F32),
        compiler_params=_params(vmem, 2),
        name="modulation",
    )(c_all, ada_w, ada_b.reshape(DEPTH, 1, n_cols))


def _norm_kernel(l_ref, x_ref, gain_ref, sc_ref, sh_ref, o_ref, *, tm):
    rc = _chunk_rows(D_MODEL)

    def body(r0):
        o_ref[pl.ds(r0, rc), :] = _adaln(x_ref[pl.ds(r0, rc), :], gain_ref[...], _rows(sc_ref, r0, rc),
                                         _rows(sh_ref, r0, rc)).astype(BF16)
    _for_chunks(tm, rc, body)


def _norm(grp, l_arr, x, gain, mod):
    tm = grp.tm
    row = pl.BlockSpec((tm, D_MODEL), lambda i, l: (i, 0))
    in_specs = [row, pl.BlockSpec((1, D_MODEL), lambda i, l: (0, 0)),
                grp.mod_spec(SC1, 0), grp.mod_spec(SH1, 0)]
    return _call(functools.partial(_norm_kernel, tm=tm), grp.name, "norm_mod", (grp.n_tiles,),
                 in_specs, row, jax.ShapeDtypeStruct((grp.rows, D_MODEL), BF16), [],
                 6 * tm * D_MODEL * 4 + (8 << 20), l_arr, [x, gain, mod, mod])


def _hist_begin(xx_ref, hp, tm, t_in_seq):
    @pl.when(t_in_seq == 0)
    def _():
        xx_ref[pl.ds(0, hp), :] = jnp.zeros((hp, xx_ref.shape[1]), F32)

    @pl.when(t_in_seq != 0)
    def _():
        xx_ref[pl.ds(0, hp), :] = xx_ref[pl.ds(tm, hp), :]


def _tap_rowmajor(xx_ref, hp, k_hist, r0, rc, cols=slice(None)):
    return lambda j: xx_ref[pl.ds(r0 + hp - (k_hist - j), rc), cols]


def _tap_timemajor(st_ref, new_ref, n_seq, k_hist, t, cols=slice(None)):
    def tap(j):
        step = t + j
        if step < k_hist:
            return st_ref[pl.ds(step * n_seq, n_seq), cols]
        return new_ref[pl.ds((step - k_hist) * n_seq, n_seq), cols]
    return tap


def _dwconv(tap, w_ref, k_w):
    acc = None
    for k in range(k_w):
        term = tap(k) * w_ref[k:k + 1, :]
        acc = term if acc is None else acc + term
    return acc


def _tail_store(grp, tail_ref, xx_ref, hp, k_hist, t_in_seq):
    r = grp.tail_rows(k_hist)

    @pl.when(t_in_seq == grp.tps - 1)
    def _():
        tail_ref[...] = xx_ref[pl.ds(hp + grp.tm - r, r), :]


def _tile_chunks(grp, width):
    if grp.time_major:
        assert grp.n_seq * width <= 4 * ACC_VREGS * V7X_VREG_ELEMS
        return [(t * grp.n_seq, grp.n_seq, t) for t in range(grp.t_len)]
    rc = _chunk_rows(width)
    return [(r0, rc, None) for r0 in range(0, grp.tm, rc)]


def _branch_a_kernel(l_ref, *refs, grp, tc):
    if grp.time_major:
        h_ref, wb_ref, wc_ref, wv_ref, cw_ref, st_ref, ya_ref, new_ref, bg_ref, w16_ref = refs
    else:
        h_ref, wb_ref, wc_ref, wv_ref, cw_ref, ya_ref, tail_ref, xx_ref, bg_ref, w16_ref = refs
    tm, k_hist = grp.tm, CONV_A - 1

    @pl.when(pl.program_id(1) == 0)
    def _():
        for n, w_ref in enumerate((wb_ref, wc_ref, wv_ref)):
            w16_ref[n] = w_ref[...].astype(BF16)

    h = h_ref[...]
    bg_ref[...] = _dot(h, w16_ref[0])
    cv = _dot(h, w16_ref[1]) * _dot(h, w16_ref[2])
    if grp.time_major:
        new_ref[...] = cv
    else:
        t_in_seq = pl.program_id(1) % grp.tps
        hp = grp.hist_rows(k_hist)
        _hist_begin(xx_ref, hp, tm, t_in_seq)
        xx_ref[pl.ds(hp, tm), :] = cv
        _tail_store(grp, tail_ref, xx_ref, hp, k_hist, t_in_seq)
    for r0, rc, t in _tile_chunks(grp, tc):
        if grp.time_major:
            tap = _tap_timemajor(st_ref, new_ref, grp.n_seq, k_hist, t)
        else:
            tap = _tap_rowmajor(xx_ref, hp, k_hist, r0, rc)
        ya_ref[pl.ds(r0, rc), :] = (bg_ref[pl.ds(r0, rc), :] * _dwconv(tap, cw_ref, CONV_A)).astype(BF16)


def _branch_a(grp, l_arr, h, w_in, conv_w, state):
    tm, tc, k_hist = grp.tm, COL_TILE, CONV_A - 1
    nq = D_A // tc
    off_c, off_v = D_A // tc, 2 * D_A // tc
    in_specs = [
        pl.BlockSpec((tm, D_MODEL), lambda q, i, l: (i, 0)),
        pl.BlockSpec((None, D_MODEL, tc), lambda q, i, l: (l[0], 0, q)),
        pl.BlockSpec((None, D_MODEL, tc), lambda q, i, l: (l[0], 0, off_c + q)),
        pl.BlockSpec((None, D_MODEL, tc), lambda q, i, l: (l[0], 0, off_v + q)),
        pl.BlockSpec((None, CONV_A, tc), lambda q, i, l: (l[0], 0, q)),
    ]
    args = [h, w_in, w_in, w_in, conv_w]
    scratch = []
    if grp.time_major:
        in_specs.append(pl.BlockSpec((None, k_hist * grp.n_seq, tc), lambda q, i, l: (l[0], 0, q)))
        args.append(state)
        tail_spec = pl.BlockSpec((tm, tc), lambda q, i, l: (0, q))
        tail_shape = jax.ShapeDtypeStruct((tm, D_A), F32)
    else:
        tps = grp.tps
        r = grp.tail_rows(k_hist)
        scratch.append(pltpu.VMEM((grp.hist_rows(k_hist) + tm, tc), F32))
        tail_spec = pl.BlockSpec((None, r, tc), lambda q, i, l: (i // tps, 0, q))
        tail_shape = jax.ShapeDtypeStruct((grp.n_seq, r, D_A), F32)
    scratch.append(pltpu.VMEM((tm, tc), F32))
    scratch.append(pltpu.VMEM((3, D_MODEL, tc), BF16))
    vmem = (2 * (tm * D_MODEL * 2 + 3 * D_MODEL * tc * 4 + tm * tc * 2 + tm * tc * 4) + 3 * D_MODEL * tc * 2
            + 10 * tm * tc * 4 + (6 << 20))
    return _call(functools.partial(_branch_a_kernel, grp=grp, tc=tc), grp.name, "branch_a",
                 (nq, grp.n_tiles), in_specs,
                 [pl.BlockSpec((tm, tc), lambda q, i, l: (i, q)), tail_spec],
                 [jax.ShapeDtypeStruct((grp.rows, D_A), BF16), tail_shape], scratch, vmem, l_arr, args)


def _branch_b_kernel(l_ref, *refs, grp):
    if grp.time_major:
        h_ref, w_ref, st_ref, diff_ref, new_ref, w16_ref = refs
    else:
        h_ref, w_ref, diff_ref, tail_ref, xx_ref, w16_ref = refs
    tm, k_hist = grp.tm, POOL_BUF

    @pl.when(pl.program_id(0) == 0)
    def _():
        w16_ref[...] = w_ref[...].astype(BF16)

    u = _dot(h_ref[...], w16_ref[...])
    if grp.time_major:
        new_ref[...] = u
    else:
        t_in_seq = pl.program_id(0) % grp.tps
        hp = grp.hist_rows(k_hist)
        _hist_begin(xx_ref, hp, tm, t_in_seq)
        xx_ref[pl.ds(hp, tm), :] = u
        _tail_store(grp, tail_ref, xx_ref, hp, k_hist, t_in_seq)
    for g, win in enumerate(POOL_WINDOWS):
        cols = slice(g * POOL_IN, (g + 1) * POOL_IN)
        for r0, rc, t in _tile_chunks(grp, POOL_IN):
            if grp.time_major:
                tap = _tap_timemajor(st_ref, new_ref, grp.n_seq, k_hist, t, cols)
                assert grp.p0 + 1 >= win
                cnt = float(win)
            else:
                tap = _tap_rowmajor(xx_ref, hp, k_hist, r0, rc, cols)
                pos = grp.p0 + t_in_seq * tm + r0 + lax.broadcasted_iota(jnp.int32, (rc, 1), 0)
                cnt = jnp.minimum(pos + 1, win).astype(F32)
            cur = tap(k_hist)
            total = cur
            for j in range(1, win):
                total = total + tap(k_hist - j)
            diff_ref[pl.ds(r0, rc), cols] = (total / cnt - cur).astype(BF16)


def _branch_b(grp, l_arr, h, w_in, state):
    tm, k_hist = grp.tm, POOL_BUF
    off = 3 * D_A // D_B
    in_specs = [
        pl.BlockSpec((tm, D_MODEL), lambda i, l: (i, 0)),
        pl.BlockSpec((None, D_MODEL, D_B), lambda i, l: (l[0], 0, off)),
    ]
    args = [h, w_in]
    scratch = []
    if grp.time_major:
        in_specs.append(pl.BlockSpec((None, k_hist * grp.n_seq, D_B), lambda i, l: (l[0], 0, 0)))
        args.append(state)
        tail_spec = pl.BlockSpec((tm, D_B), lambda i, l: (0, 0))
        tail_shape = jax.ShapeDtypeStruct((tm, D_B), F32)
        state_bytes = 2 * k_hist * grp.n_seq * D_B * 4
    else:
        tps = grp.tps
        r = grp.tail_rows(k_hist)
        scratch.append(pltpu.VMEM((grp.hist_rows(k_hist) + tm, D_B), F32))
        tail_spec = pl.BlockSpec((None, r, D_B), lambda i, l: (i // tps, 0, 0))
        tail_shape = jax.ShapeDtypeStruct((grp.n_seq, r, D_B), F32)
        state_bytes = 0
    scratch.append(pltpu.VMEM((D_MODEL, D_B), BF16))
    vmem = (2 * (tm * D_MODEL * 2 + D_MODEL * D_B * 4 + tm * D_B * 2 + tm * D_B * 4) + D_MODEL * D_B * 2
            + 8 * tm * D_B * 4 + state_bytes + (6 << 20))
    return _call(functools.partial(_branch_b_kernel, grp=grp), grp.name, "branch_b",
                 (grp.n_tiles,), in_specs,
                 [pl.BlockSpec((tm, D_B), lambda i, l: (i, 0)), tail_spec],
                 [jax.ShapeDtypeStruct((grp.rows, D_B), BF16), tail_shape], scratch, vmem, l_arr, args)


def _branch_c_kernel(l_ref, *refs, grp):
    if grp.time_major:
        h_ref, w1_ref, w2_ref, cw_ref, cb_ref, lg_ref, lb_ref, st_ref, z_ref, new_ref = refs
    else:
        h_ref, w1_ref, w2_ref, cw_ref, cb_ref, lg_ref, lb_ref, z_ref, tail_ref, xs_ref, v_ref, cwb_ref = refs
    tm, k_hist = grp.tm, CONV_C - 1

    def norm_swish(v):
        mu = jnp.mean(v, axis=-1, keepdims=True)
        vc = v - mu
        var = jnp.mean(vc * vc, axis=-1, keepdims=True)
        y = vc * lax.rsqrt(var + EPS) * lg_ref[...] + lb_ref[...]
        return (y * jax.nn.sigmoid(y)).astype(BF16)

    h = h_ref[...]
    glu = _dot(h, w1_ref[...]) * jax.nn.sigmoid(_dot(h, w2_ref[...]))
    if grp.time_major:
        new_ref[...] = glu
    else:
        t_in_seq = pl.program_id(0) % grp.tps
        hp = grp.hist_rows(k_hist)
        x0 = xs_ref.at[0]
        _hist_begin(x0, hp, tm, t_in_seq)
        x0[pl.ds(hp, tm), :] = glu
        _tail_store(grp, tail_ref, x0, hp, k_hist, t_in_seq)
        for b in range(1, V7X_SUBLANES):
            xs_ref[b, pl.ds(0, hp), :] = x0[pl.ds(b, hp), :]
            xs_ref[b, pl.ds(hp - b, tm), :] = glu

        rc = _chunk_rows(D_C)

        @pl.when(pl.program_id(0) == 0)
        def _():
            for k in range(CONV_C):
                cwb_ref[k] = jnp.broadcast_to(cw_ref[k:k + 1, :], (V7X_SUBLANES, D_C))

        def conv_chunk(r0):
            groups = range(0, rc, V7X_SUBLANES)
            accs = {}
            for k in range(CONV_C):
                a, b = divmod(hp - k_hist + k, V7X_SUBLANES)
                w_k = cwb_ref[k]
                for g in groups:
                    term = xs_ref[b, pl.ds(r0 + V7X_SUBLANES * a + g, V7X_SUBLANES), :] * w_k
                    accs[g] = term if k == 0 else accs[g] + term
            for g in groups:
                v_ref[pl.ds(r0 + g, V7X_SUBLANES), :] = accs[g] + cb_ref[...]
        _for_chunks(tm, rc, conv_chunk)

        def norm_chunks(r_base):
            for u in range(EPILOGUE_UNROLL):
                r0 = r_base + u * rc
                z_ref[pl.ds(r0, rc), :] = norm_swish(v_ref[pl.ds(r0, rc), :])
        _for_chunks(tm, EPILOGUE_UNROLL * rc, norm_chunks)
        return
    for r0, rc, t in _tile_chunks(grp, D_C):
        tap = _tap_timemajor(st_ref, new_ref, grp.n_seq, k_hist, t)
        z_ref[pl.ds(r0, rc), :] = norm_swish(_dwconv(tap, cw_ref, CONV_C) + cb_ref[...])


def _branch_c(grp, l_arr, h, w_in, conv_w, conv_b, ln_g, ln_b, state):
    tm, k_hist = grp.tm, CONV_C - 1
    off1 = (3 * D_A + D_B - W_IN_CG_START) // D_C
    vec = pl.BlockSpec((None, 1, D_C), lambda i, l: (l[0], 0, 0))
    in_specs = [
        pl.BlockSpec((tm, D_MODEL), lambda i, l: (i, 0)),
        pl.BlockSpec((None, D_MODEL, D_C), lambda i, l: (l[0], 0, off1)),
        pl.BlockSpec((None, D_MODEL, D_C), lambda i, l: (l[0], 0, off1 + 1)),
        pl.BlockSpec((None, CONV_C, D_C), lambda i, l: (l[0], 0, 0)),
        vec, vec, vec,
    ]
    args = [h, w_in, w_in, conv_w, conv_b, ln_g, ln_b]
    scratch = []
    if grp.time_major:
        in_specs.append(pl.BlockSpec((None, k_hist * grp.n_seq, D_C), lambda i, l: (l[0], 0, 0),
                                     pipeline_mode=pl.Buffered(1)))
        args.append(state)
        tail_spec = pl.BlockSpec((tm, D_C), lambda i, l: (0, 0))
        tail_shape = jax.ShapeDtypeStruct((tm, D_C), F32)
        state_bytes = k_hist * grp.n_seq * D_C * 4
    else:
        tps = grp.tps
        r = grp.tail_rows(k_hist)
        shifted_rows = grp.hist_rows(k_hist) + tm + V7X_SUBLANES
        scratch.append(pltpu.VMEM((V7X_SUBLANES, shifted_rows, D_C), F32))
        scratch.append(pltpu.VMEM((tm, D_C), F32))
        scratch.append(pltpu.VMEM((CONV_C, V7X_SUBLANES, D_C), F32))
        tail_spec = pl.BlockSpec((None, r, D_C), lambda i, l: (i // tps, 0, 0))
        tail_shape = jax.ShapeDtypeStruct((grp.n_seq, r, D_C), F32)
        state_bytes = V7X_SUBLANES * shifted_rows * D_C * 4
    vmem = (2 * (tm * D_MODEL * 2 + 2 * D_MODEL * D_C * 2 + tm * D_C * 2 + tm * D_C * 4)
            + 10 * tm * D_C * 4 + state_bytes + (6 << 20))
    return _call(functools.partial(_branch_c_kernel, grp=grp), grp.name, "branch_c",
                 (grp.n_tiles,), in_specs,
                 [pl.BlockSpec((tm, D_C), lambda i, l: (i, 0)), tail_spec],
                 [jax.ShapeDtypeStruct((grp.rows, D_C), BF16), tail_shape], scratch, vmem, l_arr, args)


def _residual_epilogue(x_ref, gate_ref, acc_ref, ng_ref, sc_ref, sh_ref, xo_ref, ho_ref, tm):
    rc = _chunk_rows(D_MODEL)

    def body(r_base):
        for u in range(EPILOGUE_UNROLL):
            r0 = r_base + u * rc
            x = x_ref[pl.ds(r0, rc), :] + _rows(gate_ref, r0, rc) * acc_ref[pl.ds(r0, rc), :]
            xo_ref[pl.ds(r0, rc), :] = x
            ho_ref[pl.ds(r0, rc), :] = _adaln(x, ng_ref[...], _rows(sc_ref, r0, rc),
                                              _rows(sh_ref, r0, rc)).astype(BF16)
    _for_chunks(tm, EPILOGUE_UNROLL * rc, body)


def _merge_kernel(l_ref, h_ref, ya_ref, df_ref, z_ref, woa_ref, pw_ref, ps_ref, wp2_ref, bp2_ref,
                  wg0_ref, wg1_ref, wg2_ref, m_ref):
    h, ya, df, z = h_ref[...], ya_ref[...], df_ref[...], z_ref[...]
    for c0 in range(0, m_ref.shape[1], MERGE_SUB_TILE):
        cols = slice(c0, c0 + MERGE_SUB_TILE)
        y_a = _dot(ya, woa_ref[:, cols])
        y_b = _dot(df, pw_ref[:, cols]) * ps_ref[:, cols]
        y_c = _dot(z, wp2_ref[:, cols]) + bp2_ref[:, cols]
        merged = (jax.nn.sigmoid(_dot(h, wg0_ref[:, cols])) * y_a
                  + jax.nn.sigmoid(_dot(h, wg1_ref[:, cols])) * y_b
                  + jax.nn.sigmoid(_dot(h, wg2_ref[:, cols])) * y_c)
        m_ref[:, cols] = merged.astype(BF16)


def _merge(grp, l_arr, h, ya, diffs, z, w_in, w_out_a, pool_w, pool_scale, w_pw2, b_pw2):
    tm, tn = grp.tm, POOL_OUT
    nj = D_MODEL // tn
    off_g = (3 * D_A + D_B + 2 * D_C - W_IN_CG_START) // tn
    per_g = D_MODEL // tn
    row = lambda c: pl.BlockSpec((tm, c), lambda j, i, l: (i, 0))
    in_specs = [
        row(D_MODEL), row(D_A),
        pl.BlockSpec((tm, POOL_IN), lambda j, i, l: (i, j)),
        row(D_C),
        pl.BlockSpec((None, D_A, tn), lambda j, i, l: (l[0], 0, j)),
        pl.BlockSpec((None, None, POOL_IN, tn), lambda j, i, l: (l[0], j, 0, 0)),
        pl.BlockSpec((None, 1, tn), lambda j, i, l: (l[0], 0, j)),
        pl.BlockSpec((None, D_C, tn), lambda j, i, l: (l[0], 0, j)),
        pl.BlockSpec((None, 1, tn), lambda j, i, l: (l[0], 0, j)),
        pl.BlockSpec((None, D_MODEL, tn), lambda j, i, l: (l[0], 0, off_g + j)),
        pl.BlockSpec((None, D_MODEL, tn), lambda j, i, l: (l[0], 0, off_g + per_g + j)),
        pl.BlockSpec((None, D_MODEL, tn), lambda j, i, l: (l[0], 0, off_g + 2 * per_g + j)),
    ]
    act_bytes = tm * (D_MODEL + D_A + POOL_IN + D_C + tn) * 2
    w_bytes = (D_A + POOL_IN + D_C + 3 * D_MODEL) * tn * 2
    vmem = 2 * (act_bytes + w_bytes) + 16 * tm * tn * 4 + (4 << 20)
    args = [h, ya, diffs, z, w_out_a, pool_w, pool_scale, w_pw2, b_pw2, w_in, w_in, w_in]
    return _call(_merge_kernel, grp.name, "merge", (nj, grp.n_tiles), in_specs,
                 pl.BlockSpec((tm, tn), lambda j, i, l: (i, j)),
                 jax.ShapeDtypeStruct((grp.rows, D_MODEL), BF16), [], vmem, l_arr, args)


def _out_proj_kernel(l_ref, m_ref, x_ref, g1_ref, sc2_ref, sh2_ref, ng_ref, wo_ref,
                     xo_ref, ho_ref, acc_ref, *, tm):
    acc_ref[...] = _dot(m_ref[...], wo_ref[...])
    _residual_epilogue(x_ref, g1_ref, acc_ref, ng_ref, sc2_ref, sh2_ref, xo_ref, ho_ref, tm)


def _out_proj(grp, l_arr, merged, x, mod, norm_g, w_o):
    tm = grp.tm
    row = pl.BlockSpec((tm, D_MODEL), lambda i, l: (i, 0))
    in_specs = [
        row, row, grp.mod_spec(G1, 0), grp.mod_spec(SC2, 0), grp.mod_spec(SH2, 0),
        pl.BlockSpec((None, 1, D_MODEL), lambda i, l: (l[0], 0, 0)),
        pl.BlockSpec((None, D_MODEL, D_MODEL), lambda i, l: (l[0], 0, 0), pipeline_mode=pl.Buffered(1)),
    ]
    vmem = (2 * tm * D_MODEL * (2 + 4 + 4 + 2) + D_MODEL * D_MODEL * 2 + 3 * tm * D_MODEL * 4
            + (4 << 20))
    return _call(functools.partial(_out_proj_kernel, tm=tm), grp.name, "out_proj",
                 (grp.n_tiles,), in_specs, [row, row],
                 [jax.ShapeDtypeStruct((grp.rows, D_MODEL), F32),
                  jax.ShapeDtypeStruct((grp.rows, D_MODEL), BF16)],
                 [pltpu.VMEM((tm, D_MODEL), F32)], vmem, l_arr, [merged, x, mod, mod, mod, norm_g, w_o])


def _final_epilogue(x_ref, gate_ref, acc_ref, ng_ref, y_ref, tm):
    rc = _chunk_rows(D_MODEL)

    def body(r_base):
        for u in range(EPILOGUE_UNROLL):
            r0 = r_base + u * rc
            x = x_ref[pl.ds(r0, rc), :] + _rows(gate_ref, r0, rc) * acc_ref[pl.ds(r0, rc), :]
            y_ref[pl.ds(r0, rc), :] = (
                x * lax.rsqrt(jnp.mean(x * x, axis=-1, keepdims=True) + EPS) * ng_ref[...])
    _for_chunks(tm, EPILOGUE_UNROLL * rc, body)


def _ffn_kernel(l_ref, h_ref, x_ref, g2_ref, wg_ref, wu_ref, wd_ref, ng_ref, *rest, tm, nk, last):
    if last:
        y_ref, acc_ref = rest
    else:
        sc_ref, sh_ref, xo_ref, ho_ref, acc_ref = rest
    k = pl.program_id(1)

    @pl.when(k == 0)
    def _():
        acc_ref[...] = jnp.zeros_like(acc_ref)

    h = h_ref[...]
    tn = wg_ref.shape[1]
    part = None
    for c0 in range(0, tn, FFN_SUB_TILE):
        cols = slice(c0, c0 + FFN_SUB_TILE)
        gate = _dot(h, wg_ref[:, cols])
        act = (gate * jax.nn.sigmoid(gate)) * _dot(h, wu_ref[:, cols])
        p = _dot(act.astype(BF16), wd_ref[cols, :])
        part = p if part is None else part + p
    acc_ref[...] += part

    @pl.when(k == nk - 1)
    def _():
        if last:
            _final_epilogue(x_ref, g2_ref, acc_ref, ng_ref, y_ref, tm)
        else:
            _residual_epilogue(x_ref, g2_ref, acc_ref, ng_ref, sc_ref, sh_ref, xo_ref, ho_ref, tm)


def _ffn(grp, l_arr, h, x, mod, w_gate_up, w_down, gain, last):
    tm, tn = grp.tm, COL_TILE
    nk = D_FF // tn
    row = pl.BlockSpec((tm, D_MODEL), lambda i, k, l: (i, 0))
    in_specs = [
        row, row, grp.mod_spec(G2, 0),
        pl.BlockSpec((None, D_MODEL, tn), lambda i, k, l: (l[0], 0, k)),
        pl.BlockSpec((None, D_MODEL, tn), lambda i, k, l: (l[0], 0, nk + k)),
        pl.BlockSpec((None, tn, D_MODEL), lambda i, k, l: (l[0], k, 0)),
    ]
    args = [h, x, mod, w_gate_up, w_gate_up, w_down, gain]
    if last:
        in_specs.append(pl.BlockSpec((1, D_MODEL), lambda i, k, l: (0, 0)))
        out_specs = [row]
        out_shape = [jax.ShapeDtypeStruct((grp.rows, D_MODEL), F32)]
    else:
        in_specs += [pl.BlockSpec((None, 1, D_MODEL), lambda i, k, l: (l[0] + 1, 0, 0)),
                     grp.next_mod_spec(SC1, 0), grp.next_mod_spec(SH1, 0)]
        args += [mod, mod]
        out_specs = [row, row]
        out_shape = [jax.ShapeDtypeStruct((grp.rows, D_MODEL), F32),
                     jax.ShapeDtypeStruct((grp.rows, D_MODEL), BF16)]
    vmem = (2 * (tm * D_MODEL * (2 + 4) + 3 * D_MODEL * tn * 2 + tm * D_MODEL * (4 + 2))
            + tm * D_MODEL * 4 + 10 * tm * tn * 4 + (4 << 20))
    return _call(functools.partial(_ffn_kernel, tm=tm, nk=nk, last=last), grp.name,
                 "ffn_last" if last else "ffn", (grp.n_tiles, nk), in_specs, out_specs, out_shape,
                 [pltpu.VMEM((tm, D_MODEL), F32)], vmem, l_arr, args)


def _layer(grp, l_arr, x, h, mod, states, p, last):
    st_a, st_p, st_c = states if states is not None else (None, None, None)
    ya, tail_a = _branch_a(grp, l_arr, h, p["w_in_f32"], p["conv_a_w"], st_a)
    diffs, tail_p = _branch_b(grp, l_arr, h, p["w_in_f32"], st_p)
    z, tail_c = _branch_c(grp, l_arr, h, p["w_in_cg"], p["conv_c_w"], p["conv_c_b"], p["ln_c_g"],
                          p["ln_c_b"], st_c)
    merged = _merge(grp, l_arr, h, ya, diffs, z, p["w_in_cg"], p["w_out_a"], p["pool_w"],
                    p["pool_scale"], p["w_pw2"], p["b_pw2"])
    x, h = _out_proj(grp, l_arr, merged, x, mod, p["norm_ffn_g"], p["w_o"])
    tails = (tail_a, tail_p, tail_c)
    if last:
        (y,) = _ffn(grp, l_arr, h, x, mod, p["w_gate_up"], p["w_down"], p["final_norm_g"], True)
        return y, tails
    x, h = _ffn(grp, l_arr, h, x, mod, p["w_gate_up"], p["w_down"], p["norm_mix_g"], False)
    return x, h, tails


def kernel(x_prompt, x_sample, c_prompt, c_sample, state_conv_a, state_pool, state_conv_c,
           ada_w, ada_b, norm_mix_g, w_in, conv_a_w, w_out_a, pool_w, pool_scale,
           conv_c_w, conv_c_b, ln_c_g, ln_c_b, w_pw2, b_pw2, w_o, norm_ffn_g,
           w_gate_up, w_down, final_norm_g):
    vec3 = lambda a: a.reshape(DEPTH, 1, a.shape[-1])
    p = {
        "w_in_f32": w_in, "w_in_cg": w_in[:, :, W_IN_CG_START:].astype(BF16),
        "w_out_a": w_out_a.astype(BF16), "pool_w": pool_w.astype(BF16),
        "w_pw2": w_pw2.astype(BF16), "w_o": w_o.astype(BF16), "w_gate_up": w_gate_up.astype(BF16),
        "w_down": w_down.astype(BF16),
        "conv_a_w": conv_a_w, "conv_c_w": conv_c_w,
        "conv_c_b": vec3(conv_c_b), "ln_c_g": vec3(ln_c_g), "ln_c_b": vec3(ln_c_b),
        "pool_scale": vec3(pool_scale), "b_pw2": vec3(b_pw2), "norm_ffn_g": vec3(norm_ffn_g),
        "norm_mix_g": vec3(norm_mix_g), "final_norm_g": final_norm_g.reshape(1, D_MODEL),
    }

    c_all = jnp.concatenate(
        [c_sample, c_prompt, jnp.zeros((C_ALL_ROWS - DEC_BATCH - BATCH, D_MODEL), F32)], axis=0)
    mod_s = _modulation(c_all, ada_w, ada_b)
    mod_p = mod_s[:, DEC_BATCH:DEC_BATCH + BATCH].reshape(DEPTH, BATCH, 1, 6 * D_MODEL)

    def time_major(s):
        s = jnp.swapaxes(s, -3, -2)
        return s.reshape(s.shape[:-3] + (-1, s.shape[-1]))

    states_s = (time_major(state_conv_a), time_major(state_pool), time_major(state_conv_c))
    x_p = x_prompt.reshape(BATCH * SEQ, D_MODEL)
    x_s = time_major(x_sample)

    l0 = jnp.zeros((1,), jnp.int32)
    h_p = _norm(PROMPT, l0, x_p, norm_mix_g[0:1], mod_p)
    h_s = _norm(SAMPLE, l0, x_s, norm_mix_g[0:1], mod_s)

    def body(carry, l):
        x_p, h_p, x_s, h_s = carry
        l_arr = jnp.reshape(l, (1,)).astype(jnp.int32)
        x_p, h_p, tails_p = _layer(PROMPT, l_arr, x_p, h_p, mod_p, None, p, False)
        x_s, h_s, tails_s = _layer(SAMPLE, l_arr, x_s, h_s, mod_s, states_s, p, False)
        return (x_p, h_p, x_s, h_s), (tails_p, tails_s)

    (x_p, h_p, x_s, h_s), tails = lax.scan(body, (x_p, h_p, x_s, h_s), jnp.arange(DEPTH - 1))
    l_last = jnp.full((1,), DEPTH - 1, jnp.int32)
    y_p, last_p = _layer(PROMPT, l_last, x_p, h_p, mod_p, None, p, True)
    y_s, last_s = _layer(SAMPLE, l_last, x_s, h_s, mod_s, states_s, p, True)
    tails_p, tails_s = jax.tree.map(lambda a, b: jnp.concatenate([a, b[None]], axis=0),
                                    tails, (last_p, last_s))

    def from_time_major(a):
        a = a.reshape(a.shape[:-2] + (DEC_SEQ, DEC_BATCH, a.shape[-1]))
        return jnp.swapaxes(a, -3, -2)

    hist = (CONV_A - 1, POOL_BUF, CONV_C - 1)
    old_s = (state_conv_a, state_pool, state_conv_c)
    new_p = [tails_p[b][:, :, -hist[b]:, :] for b in range(3)]
    new_s = [jnp.concatenate([old_s[b], from_time_major(tails_s[b])], axis=2)[:, :, -hist[b]:, :]
             for b in range(3)]
    return (y_p.reshape(BATCH, SEQ, D_MODEL), from_time_major(y_s),
            new_p[0], new_p[1], new_p[2], new_s[0], new_s[1], new_s[2])
```
